```python
import math
import jax, jax.numpy as jnp
from jax import lax
import numpy as np

D_MODEL = 2048
BATCH = 2
SEQ = 8192
DEPTH = 1

GRID_W = 64
CTX_LEN = 256
N_HEADS = 16
HEAD_DIM = 64
V_DIM = 2 * HEAD_DIM
ATTN_WIDTH = N_HEADS * V_DIM
CONV_CH = 1024
CONV_WIDTH = 31
N_BRANCH = 2
N_GROUPS = 4
EXPERTS_PER_GROUP = 8
N_EXPERTS = N_GROUPS * EXPERTS_PER_GROUP
TOP_K_IN_GROUP = 2
EXPERT_FF = 512
Q_BLOCK = 128
ROPE_BASE = 10000.0
ROPE_AXIS_DIM = HEAD_DIM // 2
N_MOD = 6
EPS = 1e-6

Q_OFF = 0
K_OFF = Q_OFF + N_HEADS * 2 * HEAD_DIM
V_OFF = K_OFF + N_HEADS * 2 * HEAD_DIM
CONV_OFF = V_OFF + ATTN_WIDTH
GATE_OFF = CONV_OFF + 2 * CONV_CH
IN_COLS = GATE_OFF + N_BRANCH * D_MODEL

kernel_name = "hybrid_diffattn_conformer_hmoe_dit"


def rmsnorm(x, g):
    xf = x.astype(jnp.float32)
    y = xf * lax.rsqrt(jnp.mean(xf * xf, axis=-1, keepdims=True) + EPS)
    return (y * g.astype(jnp.float32)).astype(x.dtype)


def layernorm(x, g, b):
    xf = x.astype(jnp.float32)
    mu = jnp.mean(xf, axis=-1, keepdims=True)
    var = jnp.mean(jnp.square(xf - mu), axis=-1, keepdims=True)
    y = (xf - mu) * lax.rsqrt(var + EPS)
    return (y * g.astype(jnp.float32) + b.astype(jnp.float32)).astype(x.dtype)


def modulate(h, shift, scale):
    return h * (1.0 + scale) + shift


def ada_params(cond, w_ada, b_ada):
    m = jax.nn.silu(cond) @ w_ada + b_ada
    return m.reshape(m.shape[:-1] + (N_MOD, D_MODEL))


def axial_rope_tables(n_tokens):
    rows = n_tokens // GRID_W
    row_pos = jnp.repeat(jnp.arange(rows), GRID_W).astype(jnp.float32)
    col_pos = jnp.tile(jnp.arange(GRID_W), rows).astype(jnp.float32)
    inv = ROPE_BASE ** (-jnp.arange(0, ROPE_AXIS_DIM, 2, dtype=jnp.float32) / ROPE_AXIS_DIM)
    ang_r = row_pos[:, None] * inv
    ang_c = col_pos[:, None] * inv
    ang = jnp.concatenate([ang_r, ang_r, ang_c, ang_c], axis=-1)
    return jnp.cos(ang), jnp.sin(ang)


def rotate_pairs(x):
    a, b, c, d = jnp.split(x, 4, axis=-1)
    return jnp.concatenate([-b, a, -d, c], axis=-1)


def apply_rope(x, cos, sin):
    cos = cos[:, None, None, :].astype(x.dtype)
    sin = sin[:, None, None, :].astype(x.dtype)
    return x * cos + rotate_pairs(x) * sin


def split_proj(p):
    b, n = p.shape[:2]
    q = p[..., Q_OFF:K_OFF].reshape(b, n, N_HEADS, 2, HEAD_DIM)
    k = p[..., K_OFF:V_OFF].reshape(b, n, N_HEADS, 2, HEAD_DIM)
    v = p[..., V_OFF:CONV_OFF].reshape(b, n, N_HEADS, V_DIM)
    u = p[..., CONV_OFF:GATE_OFF]
    gates = jax.nn.sigmoid(p[..., GATE_OFF:IN_COLS].reshape(b, n, N_BRANCH, D_MODEL))
    return q, k, v, u, gates


def diff_attention_block(q, k, v, lam):
    s = jnp.einsum('bqhmd,bkhmd->bhmqk', q, k).astype(jnp.float32)
    p = jax.nn.softmax(s, axis=-1)
    a = p[:, :, 0] - lam * p[:, :, 1]
    return jnp.einsum('bhqk,bkhe->bqhe', a.astype(v.dtype), v)


def latent_diff_attention(q, k_all, v_all, lam):
    b, s = q.shape[:2]
    nb = s // Q_BLOCK
    qb = q.reshape(b, nb, Q_BLOCK, N_HEADS, 2, HEAD_DIM).swapaxes(0, 1)
    ob = lax.map(lambda qq: diff_attention_block(qq, k_all, v_all, lam), qb)
    return ob.swapaxes(0, 1).reshape(b, s, N_HEADS, V_DIM)


def head_out(o, g, lambda_init):
    o = rmsnorm(o, g) * (1.0 - lambda_init)
    return o.reshape(o.shape[:2] + (ATTN_WIDTH,))


def conformer_conv(u, dw_w, dw_b, ln_g, ln_b):
    a, gt = jnp.split(u, 2, axis=-1)
    h = a * jax.nn.sigmoid(gt)
    h = lax.conv_general_dilated(
        h, dw_w[:, None, :], window_strides=(1,),
        padding=[(CONV_WIDTH // 2, CONV_WIDTH // 2)],
        dimension_numbers=('NWC', 'WIO', 'NWC'),
        feature_group_count=CONV_CH) + dw_b
    h = layernorm(h, ln_g, ln_b)
    return jax.nn.silu(h)


def merge_branches(o_attn, h_conv, gates, w_attn_proj, w_conv_proj, w_out):
    merged = gates[..., 0, :] * (o_attn @ w_attn_proj) + gates[..., 1, :] * (h_conv @ w_conv_proj)
    return merged @ w_out


def hierarchical_moe(h, w_rg, b_rg, w_re, b_re, w1, w3, w2):
    shp = h.shape
    t = h.reshape(-1, D_MODEL)
    n = t.shape[0]
    pg = jax.nn.softmax((t @ w_rg + b_rg).astype(jnp.float32), axis=-1)
    g_w, g_idx = lax.top_k(pg, 1)
    le = (t @ w_re + b_re).astype(jnp.float32).reshape(n, N_GROUPS, EXPERTS_PER_GROUP)
    idx = jnp.broadcast_to(g_idx[:, :, None], (n, 1, EXPERTS_PER_GROUP))
    le_sel = jnp.take_along_axis(le, idx, axis=1)[:, 0]
    pe = jax.nn.softmax(le_sel, axis=-1)
    e_w, e_idx = lax.top_k(pe, TOP_K_IN_GROUP)
    e_w = e_w / jnp.sum(e_w, axis=-1, keepdims=True)
    weights = g_w * e_w
    expert_id = g_idx * EXPERTS_PER_GROUP + e_idx
    combine = jnp.sum(jax.nn.one_hot(expert_id, N_EXPERTS, dtype=jnp.float32)
                      * weights[..., None], axis=1)

    def add_expert(acc, xs):
        w1e, w3e, w2e, ce = xs
        hid = jax.nn.silu(t @ w1e) * (t @ w3e)
        return acc + ce[:, None].astype(t.dtype) * (hid @ w2e), None

    y, _ = lax.scan(add_expert, jnp.zeros_like(t), (w1, w3, w2, combine.T))
    return y.reshape(shp)


def setup_inputs(seed: int = 0) -> dict:
    key = jax.random.key(seed)
    ks = jax.random.split(key, 32)
    f32 = jnp.float32
    nrm = lambda k, shape, s: jax.random.normal(k, shape, f32) * s
    d = D_MODEL
    return {
        "x": nrm(ks[0], (BATCH, SEQ, d), 1.0),
        "c": nrm(ks[1], (BATCH, d), 1.0),
        "ctx": nrm(ks[2], (BATCH, CTX_LEN, d), 1.0),
        "c_ctx": nrm(ks[3], (d,), 1.0),
        "w_ada": nrm(ks[4], (DEPTH, d, N_MOD * d), 0.5 * d ** -0.5),
        "b_ada": nrm(ks[5], (DEPTH, N_MOD * d), 0.01),
        "norm_attn_g": 1.0 + nrm(ks[6], (DEPTH, d), 0.02),
        "norm_ffn_g": 1.0 + nrm(ks[7], (DEPTH, d), 0.02),
        "w_in": nrm(ks[8], (DEPTH, d, IN_COLS), d ** -0.5),
        "lambda_q1": nrm(ks[9], (DEPTH, HEAD_DIM), 0.1),
        "lambda_k1": nrm(ks[10], (DEPTH, HEAD_DIM), 0.1),
        "lambda_q2": nrm(ks[11], (DEPTH, HEAD_DIM), 0.1),
        "lambda_k2": nrm(ks[12], (DEPTH, HEAD_DIM), 0.1),
        "attn_subln_g": 1.0 + nrm(ks[13], (DEPTH, V_DIM), 0.02),
        "conv_dw_w": nrm(ks[14], (DEPTH, CONV_WIDTH, CONV_CH), CONV_WIDTH ** -0.5),
        "conv_dw_b": nrm(ks[15], (DEPTH, CONV_CH), 0.01),
        "conv_ln_g": 1.0 + nrm(ks[16], (DEPTH, CONV_CH), 0.02),
        "conv_ln_b": nrm(ks[17], (DEPTH, CONV_CH), 0.01),
        "w_attn_proj": nrm(ks[18], (DEPTH, ATTN_WIDTH, d), ATTN_WIDTH ** -0.5),
        "w_conv_proj": nrm(ks[19], (DEPTH, CONV_CH, d), CONV_CH ** -0.5),
        "w_out": nrm(ks[20], (DEPTH, d, d), d ** -0.5),
        "router_group_w": nrm(ks[21], (DEPTH, d, N_GROUPS), d ** -0.5),
        "router_group_b": nrm(ks[22], (DEPTH, N_GROUPS), 0.01),
        "router_expert_w": nrm(ks[23], (DEPTH, d, N_EXPERTS), d ** -0.5),
        "router_expert_b": nrm(ks[24], (DEPTH, N_EXPERTS), 0.01),
        "expert_w1": nrm(ks[25], (DEPTH, N_EXPERTS, d, EXPERT_FF), d ** -0.5),
        "expert_w3": nrm(ks[26], (DEPTH, N_EXPERTS, d, EXPERT_FF), d ** -0.5),
        "expert_w2": nrm(ks[27], (DEPTH, N_EXPERTS, EXPERT_FF, d), EXPERT_FF ** -0.5),
        "final_norm_g": 1.0 + nrm(ks[28], (d,), 0.02),
    }


def reference(x, c, ctx, c_ctx, w_ada, b_ada, norm_attn_g, norm_ffn_g, w_in,
              lambda_q1, lambda_k1, lambda_q2, lambda_k2, attn_subln_g,
              conv_dw_w, conv_dw_b, conv_ln_g, conv_ln_b,
              w_attn_proj, w_conv_proj, w_out,
              router_group_w, router_group_b, router_expert_w, router_expert_b,
              expert_w1, expert_w3, expert_w2, final_norm_g):
    b, s = x.shape[:2]
    n_ctx = ctx.shape[1]
    cos, sin = axial_rope_tables(s)
    q_scale = HEAD_DIM ** -0.5
    for i in range(DEPTH):
        last = i == DEPTH - 1
        lambda_init = 0.8 - 0.6 * math.exp(-0.3 * i)
        lam = (jnp.exp(jnp.sum(lambda_q1[i] * lambda_k1[i]).astype(jnp.float32))
               - jnp.exp(jnp.sum(lambda_q2[i] * lambda_k2[i]).astype(jnp.float32))
               + lambda_init)
        m_x = ada_params(c, w_ada[i], b_ada[i])[:, :, None, :]
        m_c = ada_params(c_ctx, w_ada[i], b_ada[i])

        hx = modulate(rmsnorm(x, norm_attn_g[i]), m_x[:, 0], m_x[:, 1])
        hc = modulate(rmsnorm(ctx, norm_attn_g[i]), m_c[0], m_c[1])

        qx, kx, vx, ux, gx = split_proj(hx @ w_in[i])
        qx = apply_rope(qx, cos, sin) * q_scale
        kx = apply_rope(kx, cos, sin)

        if last:
            pc = hc @ w_in[i][:, K_OFF:CONV_OFF]
            kc = pc[..., :V_OFF - K_OFF].reshape(b, n_ctx, N_HEADS, 2, HEAD_DIM)
            vc = pc[..., V_OFF - K_OFF:].reshape(b, n_ctx, N_HEADS, V_DIM)
        else:
            qc, kc, vc, uc, gc = split_proj(hc @ w_in[i])
            oc = diff_attention_block(qc * q_scale, kc, vc, lam)
            oc = head_out(oc, attn_subln_g[i], lambda_init)
            cc = conformer_conv(uc, conv_dw_w[i], conv_dw_b[i], conv_ln_g[i], conv_ln_b[i])
            mix_c = merge_branches(oc, cc, gc, w_attn_proj[i], w_conv_proj[i], w_out[i])

        k_all = jnp.concatenate([kx, kc], axis=1)
        v_all = jnp.concatenate([vx, vc], axis=1)
        ox = latent_diff_attention(qx, k_all, v_all, lam)
        ox = head_out(ox, attn_subln_g[i], lambda_init)
        cx = conformer_conv(ux, conv_dw_w[i], conv_dw_b[i], conv_ln_g[i], conv_ln_b[i])
        mix_x = merge_branches(ox, cx, gx, w_attn_proj[i], w_conv_proj[i], w_out[i])
        x = x + m_x[:, 2] * mix_x

        hx2 = modulate(rmsnorm(x, norm_ffn_g[i]), m_x[:, 3], m_x[:, 4])
        x = x + m_x[:, 5] * hierarchical_moe(
            hx2, router_group_w[i], router_group_b[i], router_expert_w[i], router_expert_b[i],
            expert_w1[i], expert_w3[i], expert_w2[i])

        if not last:
            ctx = ctx + m_c[2] * mix_c
            hc2 = modulate(rmsnorm(ctx, norm_ffn_g[i]), m_c[3], m_c[4])
            ctx = ctx + m_c[5] * hierarchical_moe(
                hc2, router_group_w[i], router_group_b[i], router_expert_w[i], router_expert_b[i],
                expert_w1[i], expert_w3[i], expert_w2[i])

    return rmsnorm(x, final_norm_g)
```

```python
import functools
import math

import jax
import jax.numpy as jnp
from jax import lax
from jax.experimental import pallas as pl
from jax.experimental.pallas import tpu as pltpu

D_MODEL = 2048
GRID_W = 64
N_HEADS = 16
HEAD_DIM = 64
V_DIM = 2 * HEAD_DIM
ATTN_WIDTH = N_HEADS * V_DIM
CONV_CH = 1024
CONV_WIDTH = 31
N_GROUPS = 4
EXPERTS_PER_GROUP = 8
N_EXPERTS = N_GROUPS * EXPERTS_PER_GROUP
EXPERT_FF = 512
ROPE_BASE = 10000.0
ROPE_AXIS_DIM = HEAD_DIM // 2
N_MOD = 6
EPS = 1e-6

Q_OFF = 0
K_OFF = Q_OFF + N_HEADS * 2 * HEAD_DIM
V_OFF = K_OFF + N_HEADS * 2 * HEAD_DIM
CONV_OFF = V_OFF + ATTN_WIDTH
GATE_OFF = CONV_OFF + 2 * CONV_CH
IN_COLS = GATE_OFF + 2 * D_MODEL

LANES = 128
CONV_HALO = 16
VMEM_LIMIT = 56 * 1024 * 1024

F32 = jnp.float32
BF16 = jnp.bfloat16


def _params(*sem):
    return pltpu.CompilerParams(dimension_semantics=sem, vmem_limit_bytes=VMEM_LIMIT)


def _ada_kernel(c_ref, w_ref, b_ref, o_ref):
    c = c_ref[...]
    sc = c * jax.nn.sigmoid(c)
    o_ref[...] = jnp.dot(sc, w_ref[...], preferred_element_type=F32,
                         precision=lax.Precision.HIGHEST) + b_ref[...]


def _ada(cond, w, b):
    tn = 1024
    n = w.shape[1]
    return pl.pallas_call(
        _ada_kernel,
        grid=(n // tn,),
        in_specs=[pl.BlockSpec((8, D_MODEL), lambda j: (0, 0)),
                  pl.BlockSpec((D_MODEL, tn), lambda j: (0, j)),
                  pl.BlockSpec((1, tn), lambda j: (0, j))],
        out_specs=pl.BlockSpec((8, tn), lambda j: (0, j)),
        out_shape=jax.ShapeDtypeStruct((8, n), F32),
        compiler_params=_params("parallel"),
        name="ada",
    )(cond, w, b)


def _inproj_kernel(x_ref, m_ref, g_ref, w_ref, cos_ref, sa_ref, sb_ref, o_ref, hx_ref,
                   *, rope_tiles, q_tiles, tn):
    j = pl.program_id(1)

    @pl.when(j == 0)
    def _():
        x = x_ref[...]
        ms = jnp.mean(x * x, axis=-1, keepdims=True)
        y = x * lax.rsqrt(ms + EPS) * g_ref[...]
        shift = m_ref[0, 0:1, :]
        scale = m_ref[0, 1:2, :]
        hx_ref[...] = (y * (1.0 + scale) + shift).astype(BF16)

    acc = jnp.dot(hx_ref[...], w_ref[...], preferred_element_type=F32)

    if rope_tiles == 0:
        o_ref[...] = acc.astype(o_ref.dtype)
        return

    @pl.when(j < rope_tiles)
    def _():
        qs = jnp.where(j < q_tiles, HEAD_DIM ** -0.5, 1.0).astype(F32)
        cos = cos_ref[...] * qs
        sa = sa_ref[...] * qs
        sb = sb_ref[...] * qs
        for c in range(tn // LANES):
            a = acc[:, c * LANES:(c + 1) * LANES]
            r = (a * cos + pltpu.roll(a, LANES - ROPE_AXIS_DIM // 2, 1) * sa
                 + pltpu.roll(a, ROPE_AXIS_DIM // 2, 1) * sb)
            o_ref[:, c * LANES:(c + 1) * LANES] = r.astype(o_ref.dtype)

    @pl.when(j >= rope_tiles)
    def _():
        o_ref[...] = acc.astype(o_ref.dtype)


def _inproj(x2d, m, g, w, tables, *, rows_per_batch, seq, rope_cols, tm, tn):
    n, c = x2d.shape[0], w.shape[1]
    tiles_per_batch = rows_per_batch // tm
    seq_tiles = max(seq // tm, 1)
    cos, sa, sb = tables
    kern = functools.partial(_inproj_kernel, rope_tiles=rope_cols // tn,
                             q_tiles=(K_OFF - Q_OFF) // tn, tn=tn)
    tab_spec = pl.BlockSpec((tm, LANES), lambda i, j: (i % seq_tiles, 0))
    return pl.pallas_call(
        kern,
        grid=(n // tm, c // tn),
        in_specs=[pl.BlockSpec((tm, D_MODEL), lambda i, j: (i, 0)),
                  pl.BlockSpec((1, N_MOD, D_MODEL), lambda i, j: (i // tiles_per_batch, 0, 0)),
                  pl.BlockSpec((1, D_MODEL), lambda i, j: (0, 0)),
                  pl.BlockSpec((D_MODEL, tn), lambda i, j: (0, j)),
                  tab_spec, tab_spec, tab_spec],
        out_specs=pl.BlockSpec((tm, tn), lambda i, j: (i, j)),
        out_shape=jax.ShapeDtypeStruct((n, c), BF16),
        scratch_shapes=[pltpu.VMEM((tm, D_MODEL), BF16)],
        compiler_params=_params("parallel", "arbitrary"),
        name="inproj",
    )(x2d, m, g, w, cos, sa, sb)


def _rope_tables(seq):
    rows = seq // GRID_W
    row_pos = jnp.repeat(jnp.arange(rows), GRID_W).astype(F32)
    col_pos = jnp.tile(jnp.arange(GRID_W), rows).astype(F32)
    inv = ROPE_BASE ** (-jnp.arange(0, ROPE_AXIS_DIM, 2, dtype=F32) / ROPE_AXIS_DIM)
    ang_r = row_pos[:, None] * inv
    ang_c = col_pos[:, None] * inv
    ang = jnp.concatenate([ang_r, ang_r, ang_c, ang_c] * (LANES // HEAD_DIM), axis=-1)
    cos, sin = jnp.cos(ang), jnp.sin(ang)
    first_half = (jnp.arange(LANES) % ROPE_AXIS_DIM) < ROPE_AXIS_DIM // 2
    sa = jnp.where(first_half, -sin, 0.0)
    sb = jnp.where(first_half, 0.0, sin)
    return cos, sa, sb


def _attn_kernel(lam_ref, g_ref, q_ref, k_ref, v_ref, kc_ref, vc_ref, o_ref,
                 qs_ref, m_ref, l_ref, acc_ref, *, tq, tk, nkv, lambda_init):
    q = q_ref[0]
    lane = lax.broadcasted_iota(jnp.int32, (tq, LANES), 1)
    zero = jnp.zeros_like(q)
    qs_ref[0:tq, :] = jnp.where(lane < HEAD_DIM, q, zero)
    qs_ref[tq:2 * tq, :] = jnp.where(lane >= HEAD_DIM, q, zero)
    m_ref[...] = jnp.full(m_ref.shape, -jnp.inf, F32)
    l_ref[...] = jnp.zeros(l_ref.shape, F32)
    acc_ref[...] = jnp.zeros(acc_ref.shape, F32)

    def tile(k, v):
        s = lax.dot_general(qs_ref[...], k, (((1,), (1,)), ((), ())),
                            preferred_element_type=F32)
        m_prev = m_ref[...]
        m_new = jnp.maximum(m_prev, jnp.max(s, axis=1, keepdims=True))
        alpha = jnp.exp(m_prev - m_new)
        p = jnp.exp(s - m_new)
        l_ref[...] = alpha * l_ref[...] + jnp.sum(p, axis=1, keepdims=True)
        acc_ref[...] = alpha * acc_ref[...] + jnp.dot(p.astype(BF16), v, preferred_element_type=F32)
        m_ref[...] = m_new

    def body(i, carry):
        off = pl.multiple_of(i * tk, tk)
        tile(k_ref[0, pl.ds(off, tk), :], v_ref[0, pl.ds(off, tk), :])
        return carry

    lax.fori_loop(0, nkv, body, 0)
    tile(kc_ref[0], vc_ref[0])

    lam_p = lam_ref[...]
    lam = (jnp.exp(jnp.sum(lam_p[0:1] * lam_p[1:2], axis=1, keepdims=True))
           - jnp.exp(jnp.sum(lam_p[2:3] * lam_p[3:4], axis=1, keepdims=True)) + lambda_init)
    acc = acc_ref[...]
    l = l_ref[...]
    o = acc[0:tq] / l[0:tq] - lam * (acc[tq:2 * tq] / l[tq:2 * tq])
    ms = jnp.mean(o * o, axis=-1, keepdims=True)
    o = o * lax.rsqrt(ms + EPS) * g_ref[...] * (1.0 - lambda_init)
    o_ref[0] = o.astype(o_ref.dtype)


def _attention(px, pc, lam_p, g, *, lambda_init, tq, tk):
    b, s, _ = px.shape
    n_ctx = pc.shape[1]
    kblk, vblk = K_OFF // LANES, V_OFF // LANES
    kern = functools.partial(_attn_kernel, tq=tq, tk=tk, nkv=s // tk, lambda_init=lambda_init)
    return pl.pallas_call(
        kern,
        grid=(b, N_HEADS, s // tq),
        in_specs=[pl.BlockSpec((4, HEAD_DIM), lambda bb, h, i: (0, 0)),
                  pl.BlockSpec((1, V_DIM), lambda bb, h, i: (0, 0)),
                  pl.BlockSpec((1, tq, LANES), lambda bb, h, i: (bb, i, h)),
                  pl.BlockSpec((1, s, LANES), lambda bb, h, i: (bb, 0, kblk + h)),
                  pl.BlockSpec((1, s, LANES), lambda bb, h, i: (bb, 0, vblk + h)),
                  pl.BlockSpec((1, n_ctx, LANES), lambda bb, h, i: (bb, 0, h)),
                  pl.BlockSpec((1, n_ctx, LANES), lambda bb, h, i: (bb, 0, N_HEADS + h))],
        out_specs=pl.BlockSpec((1, tq, LANES), lambda bb, h, i: (bb, i, h)),
        out_shape=jax.ShapeDtypeStruct((b, s, ATTN_WIDTH), BF16),
        scratch_shapes=[pltpu.VMEM((2 * tq, LANES), BF16),
                        pltpu.VMEM((2 * tq, 1), F32),
                        pltpu.VMEM((2 * tq, 1), F32),
                        pltpu.VMEM((2 * tq, V_DIM), F32)],
        compiler_params=_params("parallel", "parallel", "arbitrary"),
        name="attn",
    )(lam_p, g, px, px, px, pc, pc)


def _conv_kernel(uc_ref, up_ref, un_ref, w_ref, b_ref, g_ref, be_ref, o_ref, h_ref, *, ts, n_tiles):
    i = pl.program_id(1)

    def glu(u):
        u = u.astype(F32)
        return u[:, :CONV_CH] * jax.nn.sigmoid(u[:, CONV_CH:])

    h_ref[0:CONV_HALO, :] = glu(up_ref[0]) * jnp.where(i > 0, 1.0, 0.0).astype(F32)
    h_ref[CONV_HALO:CONV_HALO + ts, :] = glu(uc_ref[0])
    h_ref[CONV_HALO + ts:, :] = glu(un_ref[0]) * jnp.where(i < n_tiles - 1, 1.0, 0.0).astype(F32)

    w = w_ref[...]
    base = CONV_HALO - CONV_WIDTH // 2
    acc = jnp.zeros((ts, CONV_CH), F32) + b_ref[...]
    for j in range(CONV_WIDTH):
        acc = acc + h_ref[base + j:base + j + ts, :] * w[j:j + 1, :]
    mu = jnp.mean(acc, axis=-1, keepdims=True)
    d = acc - mu
    var = jnp.mean(d * d, axis=-1, keepdims=True)
    y = d * lax.rsqrt(var + EPS) * g_ref[...] + be_ref[...]
    o_ref[0] = (y * jax.nn.sigmoid(y)).astype(o_ref.dtype)


def _conv(px, w, b, g, be, *, ts):
    bsz, s, _ = px.shape
    n_tiles = s // ts
    ublk = CONV_OFF // (2 * CONV_CH)
    hpt = ts // CONV_HALO
    n_halo = s // CONV_HALO
    kern = functools.partial(_conv_kernel, ts=ts, n_tiles=n_tiles)
    vec = pl.BlockSpec((1, CONV_CH), lambda bb, i: (0, 0))
    return pl.pallas_call(
        kern,
        grid=(bsz, n_tiles),
        in_specs=[pl.BlockSpec((1, ts, 2 * CONV_CH), lambda bb, i: (bb, i, ublk)),
                  pl.BlockSpec((1, CONV_HALO, 2 * CONV_CH),
                               lambda bb, i: (bb, jnp.maximum(i * hpt - 1, 0), ublk)),
                  pl.BlockSpec((1, CONV_HALO, 2 * CONV_CH),
                               lambda bb, i: (bb, jnp.minimum((i + 1) * hpt, n_halo - 1), ublk)),
                  pl.BlockSpec((CONV_WIDTH, CONV_CH), lambda bb, i: (0, 0)),
                  vec, vec, vec],
        out_specs=pl.BlockSpec((1, ts, CONV_CH), lambda bb, i: (bb, i, 0)),
        out_shape=jax.ShapeDtypeStruct((bsz, s, CONV_CH), BF16),
        scratch_shapes=[pltpu.VMEM((ts + 2 * CONV_HALO, CONV_CH), F32)],
        compiler_params=_params("parallel", "parallel"),
        name="conv",
    )(px, px, px, w, b, g, be)


def _split_dot(t, w_hi, w_lo):
    t_hi = t.astype(BF16)
    t_lo = (t - t_hi.astype(F32)).astype(BF16)
    return (jnp.dot(t_hi, w_hi, preferred_element_type=F32)
            + jnp.dot(t_hi, w_lo, preferred_element_type=F32)
            + jnp.dot(t_lo, w_hi, preferred_element_type=F32))


def _route(logits):
    lane = lax.broadcasted_iota(jnp.int32, logits.shape, 1)
    big = jnp.int32(1 << 20)
    neg = jnp.float32(-jnp.inf)
    is_g = (lane >= N_EXPERTS) & (lane < N_EXPERTS + N_GROUPS)
    lg = jnp.where(is_g, logits, neg)
    pgu = jnp.exp(lg - jnp.max(lg, axis=1, keepdims=True))
    pg = pgu / jnp.sum(pgu, axis=1, keepdims=True)
    g_w = jnp.max(pg, axis=1, keepdims=True)
    g_idx = jnp.min(jnp.where(is_g & (pg == g_w), lane - N_EXPERTS, big), axis=1, keepdims=True)
    sel = (lane < N_EXPERTS) & ((lane // EXPERTS_PER_GROUP) == g_idx)
    le = jnp.where(sel, logits, neg)
    peu = jnp.exp(le - jnp.max(le, axis=1, keepdims=True))
    pe = peu / jnp.sum(peu, axis=1, keepdims=True)
    v1 = jnp.max(pe, axis=1, keepdims=True)
    i1 = jnp.min(jnp.where(sel & (pe == v1), lane, big), axis=1, keepdims=True)
    rest = sel & (lane != i1)
    pe2 = jnp.where(rest, pe, -1.0)
    v2 = jnp.max(pe2, axis=1, keepdims=True)
    i2 = jnp.min(jnp.where(rest & (pe2 == v2), lane, big), axis=1, keepdims=True)
    den = v1 + v2
    return (jnp.where(lane == i1, g_w * (v1 / den), 0.0)
            + jnp.where(lane == i2, g_w * (v2 / den), 0.0))


def _merge_kernel(o_ref, cx_ref, g0_ref, g1_ref, x_ref, m_ref, ng_ref, wap_ref, wcp_ref, wout_ref,
                  wrh_ref, wrl_ref, rb_ref, x1_ref, t_ref, comb_ref):
    ya = jnp.dot(o_ref[...], wap_ref[...], preferred_element_type=F32)
    yc = jnp.dot(cx_ref[...], wcp_ref[...], preferred_element_type=F32)
    merged = (jax.nn.sigmoid(g0_ref[...].astype(F32)) * ya
              + jax.nn.sigmoid(g1_ref[...].astype(F32)) * yc)
    mix = jnp.dot(merged.astype(BF16), wout_ref[...], preferred_element_type=F32)
    x1 = x_ref[...] + m_ref[0, 2:3, :] * mix
    x1_ref[...] = x1
    ms = jnp.mean(x1 * x1, axis=-1, keepdims=True)
    y = x1 * lax.rsqrt(ms + EPS) * ng_ref[...]
    t = y * (1.0 + m_ref[0, 4:5, :]) + m_ref[0, 3:4, :]
    t_ref[...] = t.astype(BF16)
    logits = _split_dot(t, wrh_ref[...], wrl_ref[...]) + rb_ref[...]
    comb_ref[...] = _route(logits)


def _merge(o2d, cx2d, px2d, x2d, m, ng, wap, wcp, wout, wr_hi, wr_lo, rb, *, rows_per_batch, tm):
    n = x2d.shape[0]
    tiles_per_batch = rows_per_batch // tm
    gblk = GATE_OFF // D_MODEL
    row = lambda i: (i, 0)
    const = lambda i: (0, 0)

    def resident(shape):
        return pl.BlockSpec(shape, const, pipeline_mode=pl.Buffered(1))

    return pl.pallas_call(
        _merge_kernel,
        grid=(n // tm,),
        in_specs=[pl.BlockSpec((tm, ATTN_WIDTH), row),
                  pl.BlockSpec((tm, CONV_CH), row),
                  pl.BlockSpec((tm, D_MODEL), lambda i: (i, gblk)),
                  pl.BlockSpec((tm, D_MODEL), lambda i: (i, gblk + 1)),
                  pl.BlockSpec((tm, D_MODEL), row),
                  pl.BlockSpec((1, N_MOD, D_MODEL), lambda i: (i // tiles_per_batch, 0, 0)),
                  pl.BlockSpec((1, D_MODEL), const),
                  resident((ATTN_WIDTH, D_MODEL)),
                  resident((CONV_CH, D_MODEL)),
                  resident((D_MODEL, D_MODEL)),
                  resident((D_MODEL, LANES)),
                  resident((D_MODEL, LANES)),
                  pl.BlockSpec((1, LANES), const)],
        out_specs=[pl.BlockSpec((tm, D_MODEL), row),
                   pl.BlockSpec((tm, D_MODEL), row),
                   pl.BlockSpec((tm, LANES), row)],
        out_shape=[jax.ShapeDtypeStruct((n, D_MODEL), F32),
                   jax.ShapeDtypeStruct((n, D_MODEL), BF16),
                   jax.ShapeDtypeStruct((n, LANES), F32)],
        compiler_params=_params("parallel"),
        name="merge",
    )(o2d, cx2d, px2d, px2d, x2d, m, ng, wap, wcp, wout, wr_hi, wr_lo, rb)


def _moe_kernel(t_ref, comb_ref, x1_ref, m_ref, fg_ref, w1_ref, w3_ref, w2_ref, o_ref, acc_ref):
    e = pl.program_id(1)

    @pl.when(e == 0)
    def _():
        acc_ref[...] = jnp.zeros(acc_ref.shape, F32)

    t = t_ref[...]
    h1 = jnp.dot(t, w1_ref[0], preferred_element_type=F32)
    h3 = jnp.dot(t, w3_ref[0], preferred_element_type=F32)
    hid = (h1 * jax.nn.sigmoid(h1)) * h3
    comb = comb_ref[...]
    lane = lax.broadcasted_iota(jnp.int32, comb.shape, 1)
    ce = jnp.sum(jnp.where(lane == e, comb, 0.0), axis=1, keepdims=True)
    acc_ref[...] += ce * jnp.dot(hid.astype(BF16), w2_ref[0], preferred_element_type=F32)

    @pl.when(e == N_EXPERTS - 1)
    def _():
        x2 = x1_ref[...] + m_ref[0, 5:6, :] * acc_ref[...]
        ms = jnp.mean(x2 * x2, axis=-1, keepdims=True)
        o_ref[...] = x2 * lax.rsqrt(ms + EPS) * fg_ref[...]


def _moe(t2d, comb, x1, m, fg, w1, w3, w2, *, rows_per_batch, tm):
    n = t2d.shape[0]
    tiles_per_batch = rows_per_batch // tm
    row = lambda i, e: (i, 0)
    return pl.pallas_call(
        _moe_kernel,
        grid=(n // tm, N_EXPERTS),
        in_specs=[pl.BlockSpec((tm, D_MODEL), row),
                  pl.BlockSpec((tm, LANES), row),
                  pl.BlockSpec((tm, D_MODEL), row),
                  pl.BlockSpec((1, N_MOD, D_MODEL), lambda i, e: (i // tiles_per_batch, 0, 0)),
                  pl.BlockSpec((1, D_MODEL), lambda i, e: (0, 0)),
                  pl.BlockSpec((1, D_MODEL, EXPERT_FF), lambda i, e: (e, 0, 0)),
                  pl.BlockSpec((1, D_MODEL, EXPERT_FF), lambda i, e: (e, 0, 0)),
                  pl.BlockSpec((1, EXPERT_FF, D_MODEL), lambda i, e: (e, 0, 0))],
        out_specs=pl.BlockSpec((tm, D_MODEL), row),
        out_shape=jax.ShapeDtypeStruct((n, D_MODEL), F32),
        scratch_shapes=[pltpu.VMEM((tm, D_MODEL), F32)],
        compiler_params=_params("parallel", "arbitrary"),
        name="moe",
    )(t2d, comb, x1, m, fg, w1, w3, w2)


def kernel(x, c, ctx, c_ctx, w_ada, b_ada, norm_attn_g, norm_ffn_g, w_in, lambda_q1, lambda_k1, lambda_q2, lambda_k2, attn_subln_g, conv_dw_w, conv_dw_b, conv_ln_g, conv_ln_b, w_attn_proj, w_conv_proj, w_out, router_group_w, router_group_b, router_expert_w, router_expert_b, expert_w1, expert_w3, expert_w2, final_norm_g):
    b, s, d = x.shape
    n_ctx = ctx.shape[1]
    depth = w_ada.shape[0]
    assert depth == 1 and d == D_MODEL and b <= 7
    i = 0
    lambda_init = 0.8 - 0.6 * math.exp(-0.3 * i)

    cond = jnp.zeros((8, d), F32).at[:b].set(c).at[b].set(c_ctx)
    mods = _ada(cond, w_ada[i], b_ada[i][None, :])
    m_x = mods[:b].reshape(b, N_MOD, d)
    m_c = mods[b:b + 1].reshape(1, N_MOD, d)

    w_in_bf = w_in[i].astype(BF16)
    tables = _rope_tables(s)
    g_attn = norm_attn_g[i][None, :]
    px = _inproj(x.reshape(b * s, d), m_x, g_attn, w_in_bf, tables,
                 rows_per_batch=s, seq=s, rope_cols=V_OFF, tm=1024, tn=1024)
    pc = _inproj(ctx.reshape(b * n_ctx, d), m_c, g_attn, w_in_bf[:, K_OFF:CONV_OFF], tables,
                 rows_per_batch=b * n_ctx, seq=s, rope_cols=0, tm=n_ctx, tn=1024)
    px3 = px.reshape(b, s, IN_COLS)
    pc3 = pc.reshape(b, n_ctx, CONV_OFF - K_OFF)

    lam_p = jnp.stack([lambda_q1[i], lambda_k1[i], lambda_q2[i], lambda_k2[i]])
    o = _attention(px3, pc3, lam_p, attn_subln_g[i][None, :], lambda_init=lambda_init, tq=256, tk=512)
    cx = _conv(px3, conv_dw_w[i], conv_dw_b[i][None, :], conv_ln_g[i][None, :], conv_ln_b[i][None, :],
               ts=512)

    wr = jnp.zeros((d, LANES), F32)
    wr = wr.at[:, :N_EXPERTS].set(router_expert_w[i]).at[:, N_EXPERTS:N_EXPERTS + N_GROUPS].set(router_group_w[i])
    rb = jnp.zeros((1, LANES), F32)
    rb = rb.at[0, :N_EXPERTS].set(router_expert_b[i]).at[0, N_EXPERTS:N_EXPERTS + N_GROUPS].set(router_group_b[i])
    wr_hi = wr.astype(BF16)
    wr_lo = (wr - wr_hi.astype(F32)).astype(BF16)

    x1, t, comb = _merge(o.reshape(b * s, ATTN_WIDTH), cx.reshape(b * s, CONV_CH), px, x.reshape(b * s, d),
                         m_x, norm_ffn_g[i][None, :],
                         w_attn_proj[i].astype(BF16), w_conv_proj[i].astype(BF16), w_out[i].astype(BF16),
                         wr_hi, wr_lo, rb, rows_per_batch=s, tm=256)

    out = _moe(t, comb, x1, m_x, final_norm_g[None, :],
               expert_w1[i].astype(BF16), expert_w3[i].astype(BF16), expert_w2[i].astype(BF16),
               rows_per_batch=s, tm=512)
    return out.reshape(b, s, d)
```

```python
import functools
import math

import jax
import jax.numpy as jnp
from jax import lax
from jax.experimental import pallas as pl
from jax.experimental.pallas import tpu as pltpu

D_MODEL = 2048
GRID_W = 64
N_HEADS = 16
HEAD_DIM = 64
V_DIM = 2 * HEAD_DIM
ATTN_WIDTH = N_HEADS * V_DIM
CONV_CH = 1024
CONV_WIDTH = 31
N_GROUPS = 4
EXPERTS_PER_GROUP = 8
N_EXPERTS = N_GROUPS * EXPERTS_PER_GROUP
EXPERT_FF = 512
ROPE_BASE = 10000.0
ROPE_AXIS_DIM = HEAD_DIM // 2
N_MOD = 6
EPS = 1e-6

Q_OFF = 0
K_OFF = Q_OFF + N_HEADS * 2 * HEAD_DIM
V_OFF = K_OFF + N_HEADS * 2 * HEAD_DIM
CONV_OFF = V_OFF + ATTN_WIDTH
GATE_OFF = CONV_OFF + 2 * CONV_CH
IN_COLS = GATE_OFF + 2 * D_MODEL

LANES = 128
CONV_HALO = 16
VMEM_LIMIT = 56 * 1024 * 1024

F32 = jnp.float32
BF16 = jnp.bfloat16


def _params(*sem):
    return pltpu.CompilerParams(dimension_semantics=sem, vmem_limit_bytes=VMEM_LIMIT)


def _ada_kernel(c_ref, w_ref, b_ref, o_ref):
    c = c_ref[...]
    sc = c * jax.nn.sigmoid(c)
    o_ref[...] = jnp.dot(sc, w_ref[...], preferred_element_type=F32,
                         precision=lax.Precision.HIGHEST) + b_ref[...]


def _ada(cond, w, b):
    tn = 1024
    n = w.shape[1]
    return pl.pallas_call(
        _ada_kernel,
        grid=(n // tn,),
        in_specs=[pl.BlockSpec((8, D_MODEL), lambda j: (0, 0)),
                  pl.BlockSpec((D_MODEL, tn), lambda j: (0, j)),
                  pl.BlockSpec((1, tn), lambda j: (0, j))],
        out_specs=pl.BlockSpec((8, tn), lambda j: (0, j)),
        out_shape=jax.ShapeDtypeStruct((8, n), F32),
        compiler_params=_params("parallel"),
        name="ada",
    )(cond, w, b)


def _inproj_kernel(x_ref, m_ref, g_ref, w_ref, cos_ref, sa_ref, sb_ref, o_ref, hx_ref,
                   *, rope_tiles, q_tiles, tn):
    j = pl.program_id(1)

    @pl.when(j == 0)
    def _():
        x = x_ref[...]
        ms = jnp.mean(x * x, axis=-1, keepdims=True)
        y = x * lax.rsqrt(ms + EPS) * g_ref[...]
        shift = m_ref[0, 0:1, :]
        scale = m_ref[0, 1:2, :]
        hx_ref[...] = (y * (1.0 + scale) + shift).astype(BF16)

    acc = jnp.dot(hx_ref[...], w_ref[...], preferred_element_type=F32)

    if rope_tiles == 0:
        o_ref[...] = acc.astype(o_ref.dtype)
        return

    @pl.when(j < rope_tiles)
    def _():
        qs = jnp.where(j < q_tiles, HEAD_DIM ** -0.5 * math.log2(math.e), 1.0).astype(F32)
        cos = cos_ref[...] * qs
        sa = sa_ref[...] * qs
        sb = sb_ref[...] * qs
        for c in range(tn // LANES):
            a = acc[:, c * LANES:(c + 1) * LANES]
            r = (a * cos + pltpu.roll(a, LANES - ROPE_AXIS_DIM // 2, 1) * sa
                 + pltpu.roll(a, ROPE_AXIS_DIM // 2, 1) * sb)
            o_ref[:, c * LANES:(c + 1) * LANES] = r.astype(o_ref.dtype)

    @pl.when(j >= rope_tiles)
    def _():
        o_ref[...] = acc.astype(o_ref.dtype)


def _inproj(x2d, m, g, w, tables, *, rows_per_batch, seq, rope_cols, tm, tn):
    n, c = x2d.shape[0], w.shape[1]
    tiles_per_batch = rows_per_batch // tm
    seq_tiles = max(seq // tm, 1)
    cos, sa, sb = tables
    kern = functools.partial(_inproj_kernel, rope_tiles=rope_cols // tn,
                             q_tiles=(K_OFF - Q_OFF) // tn, tn=tn)
    tab_spec = pl.BlockSpec((tm, LANES), lambda i, j: (i % seq_tiles, 0))
    return pl.pallas_call(
        kern,
        grid=(n // tm, c // tn),
        in_specs=[pl.BlockSpec((tm, D_MODEL), lambda i, j: (i, 0)),
                  pl.BlockSpec((1, N_MOD, D_MODEL), lambda i, j: (i // tiles_per_batch, 0, 0)),
                  pl.BlockSpec((1, D_MODEL), lambda i, j: (0, 0)),
                  pl.BlockSpec((D_MODEL, tn), lambda i, j: (0, j)),
                  tab_spec, tab_spec, tab_spec],
        out_specs=pl.BlockSpec((tm, tn), lambda i, j: (i, j)),
        out_shape=jax.ShapeDtypeStruct((n, c), BF16),
        scratch_shapes=[pltpu.VMEM((tm, D_MODEL), BF16)],
        compiler_params=_params("parallel", "arbitrary"),
        name="inproj",
    )(x2d, m, g, w, cos, sa, sb)


def _rope_tables(seq):
    rows = seq // GRID_W
    row_pos = jnp.repeat(jnp.arange(rows), GRID_W).astype(F32)
    col_pos = jnp.tile(jnp.arange(GRID_W), rows).astype(F32)
    inv = ROPE_BASE ** (-jnp.arange(0, ROPE_AXIS_DIM, 2, dtype=F32) / ROPE_AXIS_DIM)
    ang_r = row_pos[:, None] * inv
    ang_c = col_pos[:, None] * inv
    ang = jnp.concatenate([ang_r, ang_r, ang_c, ang_c] * (LANES // HEAD_DIM), axis=-1)
    cos, sin = jnp.cos(ang), jnp.sin(ang)
    first_half = (jnp.arange(LANES) % ROPE_AXIS_DIM) < ROPE_AXIS_DIM // 2
    sa = jnp.where(first_half, -sin, 0.0)
    sb = jnp.where(first_half, 0.0, sin)
    return cos, sa, sb


def _attn_kernel(lam_ref, g_ref, qt_ref, k_ref, vt_ref, o_ref,
                 qm_ref, s_ref, p_ref, al_ref, m_ref, l_ref, acc_ref, *, tq, tk, nt, lambda_init):
    qt = qt_ref[0]
    row = lax.broadcasted_iota(jnp.int32, qt.shape, 0)
    zero = jnp.zeros_like(qt)
    qm_ref[0] = jnp.where(row < HEAD_DIM, qt, zero)
    qm_ref[1] = jnp.where(row >= HEAD_DIM, qt, zero)
    m_ref[...] = jnp.full(m_ref.shape, -jnp.inf, F32)
    l_ref[...] = jnp.zeros(l_ref.shape, F32)
    acc_ref[...] = jnp.zeros(acc_ref.shape, F32)
    al_ref[...] = jnp.ones(al_ref.shape, F32)
    p_ref[1] = jnp.zeros(p_ref.shape[1:], BF16)

    def scores(j, slot):
        off = pl.multiple_of(j * tk, tk)
        k = k_ref[0, pl.ds(off, tk), :]
        for mp in range(2):
            s_ref[slot, mp] = jnp.dot(k, qm_ref[mp], preferred_element_type=F32)

    def softmax(slot):
        for mp in range(2):
            s = s_ref[slot, mp]
            m_prev = m_ref[mp]
            m_new = jnp.maximum(m_prev, jnp.max(s, axis=0, keepdims=True))
            alpha = jnp.exp2(m_prev - m_new)
            p = jnp.exp2(s - m_new)
            l_ref[mp] = alpha * l_ref[mp] + jnp.sum(p, axis=0, keepdims=True)
            m_ref[mp] = m_new
            al_ref[slot, mp] = alpha
            p_ref[slot, mp] = p.astype(BF16)

    def pv(j, slot):
        vt = vt_ref[0, 0, j]
        for mp in range(2):
            acc_ref[mp] = (al_ref[slot, mp] * acc_ref[mp]
                           + jnp.dot(vt, p_ref[slot, mp], preferred_element_type=F32))

    scores(0, 0)

    def step(j, par, with_scores=True):
        if with_scores:
            scores(jnp.minimum(j + 1, nt - 1), 1 - par)
        softmax(par)
        pv(jnp.maximum(j - 1, 0), 1 - par)

    def body(t, carry):
        step(2 * t, 0)
        step(2 * t + 1, 1)
        return carry

    lax.fori_loop(0, nt // 2, body, 0)
    if nt % 2 == 1:
        step(nt - 1, 0, with_scores=False)
    pv(nt - 1, (nt - 1) % 2)

    lam_p = lam_ref[...]
    lam = (jnp.exp(jnp.sum(lam_p[0:1] * lam_p[1:2], axis=1, keepdims=True))
           - jnp.exp(jnp.sum(lam_p[2:3] * lam_p[3:4], axis=1, keepdims=True)) + lambda_init)
    ot = acc_ref[0] / l_ref[0] - lam * (acc_ref[1] / l_ref[1])
    ms = jnp.mean(ot * ot, axis=0, keepdims=True)
    ot = ot * lax.rsqrt(ms + EPS)
    o = ot.T * (g_ref[...] * (1.0 - lambda_init))
    o_ref[0] = o.astype(o_ref.dtype)


def _attention(qt, k_all, vt_all, lam_p, g, *, lambda_init, tq):
    b, _, s = qt.shape
    skv = k_all.shape[1]
    nt, tk = vt_all.shape[2], vt_all.shape[4]
    kern = functools.partial(_attn_kernel, tq=tq, tk=tk, nt=nt, lambda_init=lambda_init)
    return pl.pallas_call(
        kern,
        grid=(b, N_HEADS, s // tq),
        in_specs=[pl.BlockSpec((4, HEAD_DIM), lambda bb, h, i: (0, 0)),
                  pl.BlockSpec((1, V_DIM), lambda bb, h, i: (0, 0)),
                  pl.BlockSpec((1, LANES, tq), lambda bb, h, i: (bb, h, i)),
                  pl.BlockSpec((1, skv, LANES), lambda bb, h, i: (bb, 0, h)),
                  pl.BlockSpec((1, 1, nt, V_DIM, tk), lambda bb, h, i: (bb, h, 0, 0, 0))],
        out_specs=pl.BlockSpec((1, tq, LANES), lambda bb, h, i: (bb, i, h)),
        out_shape=jax.ShapeDtypeStruct((b, s, ATTN_WIDTH), BF16),
        scratch_shapes=[pltpu.VMEM((2, LANES, tq), BF16),
                        pltpu.VMEM((2, 2, tk, tq), F32),
                        pltpu.VMEM((2, 2, tk, tq), BF16),
                        pltpu.VMEM((2, 2, 1, tq), F32),
                        pltpu.VMEM((2, 1, tq), F32),
                        pltpu.VMEM((2, 1, tq), F32),
                        pltpu.VMEM((2, V_DIM, tq), F32)],
        compiler_params=_params("parallel", "parallel", "arbitrary"),
        name="attn",
    )(lam_p, g, qt, k_all, vt_all)


def _conv_kernel(uc_ref, up_ref, un_ref, w_ref, b_ref, g_ref, be_ref, o_ref, h_ref, *, ts, n_tiles):
    i = pl.program_id(1)

    def glu(u):
        u = u.astype(F32)
        return u[:, :CONV_CH] * jax.nn.sigmoid(u[:, CONV_CH:])

    h_ref[0:CONV_HALO, :] = glu(up_ref[0]) * jnp.where(i > 0, 1.0, 0.0).astype(F32)
    h_ref[CONV_HALO:CONV_HALO + ts, :] = glu(uc_ref[0])
    h_ref[CONV_HALO + ts:, :] = glu(un_ref[0]) * jnp.where(i < n_tiles - 1, 1.0, 0.0).astype(F32)

    w = w_ref[...]
    base = CONV_HALO - CONV_WIDTH // 2
    acc = jnp.zeros((ts, CONV_CH), F32) + b_ref[...]
    for j in range(CONV_WIDTH):
        acc = acc + h_ref[base + j:base + j + ts, :] * w[j:j + 1, :]
    mu = jnp.mean(acc, axis=-1, keepdims=True)
    d = acc - mu
    var = jnp.mean(d * d, axis=-1, keepdims=True)
    y = d * lax.rsqrt(var + EPS) * g_ref[...] + be_ref[...]
    o_ref[0] = (y * jax.nn.sigmoid(y)).astype(o_ref.dtype)


def _conv(px, w, b, g, be, *, ts):
    bsz, s, _ = px.shape
    n_tiles = s // ts
    ublk = CONV_OFF // (2 * CONV_CH)
    hpt = ts // CONV_HALO
    n_halo = s // CONV_HALO
    kern = functools.partial(_conv_kernel, ts=ts, n_tiles=n_tiles)
    vec = pl.BlockSpec((1, CONV_CH), lambda bb, i: (0, 0))
    return pl.pallas_call(
        kern,
        grid=(bsz, n_tiles),
        in_specs=[pl.BlockSpec((1, ts, 2 * CONV_CH), lambda bb, i: (bb, i, ublk)),
                  pl.BlockSpec((1, CONV_HALO, 2 * CONV_CH),
                               lambda bb, i: (bb, jnp.maximum(i * hpt - 1, 0), ublk)),
                  pl.BlockSpec((1, CONV_HALO, 2 * CONV_CH),
                               lambda bb, i: (bb, jnp.minimum((i + 1) * hpt, n_halo - 1), ublk)),
                  pl.BlockSpec((CONV_WIDTH, CONV_CH), lambda bb, i: (0, 0)),
                  vec, vec, vec],
        out_specs=pl.BlockSpec((1, ts, CONV_CH), lambda bb, i: (bb, i, 0)),
        out_shape=jax.ShapeDtypeStruct((bsz, s, CONV_CH), BF16),
        scratch_shapes=[pltpu.VMEM((ts + 2 * CONV_HALO, CONV_CH), F32)],
        compiler_params=_params("parallel", "parallel"),
        name="conv",
    )(px, px, px, w, b, g, be)


def _split_dot(t, w_hi, w_lo):
    t_hi = t.astype(BF16)
    t_lo = (t - t_hi.astype(F32)).astype(BF16)
    return (jnp.dot(t_hi, w_hi, preferred_element_type=F32)
            + jnp.dot(t_hi, w_lo, preferred_element_type=F32)
            + jnp.dot(t_lo, w_hi, preferred_element_type=F32))


def _route(logits):
    lane = lax.broadcasted_iota(jnp.int32, logits.shape, 1)
    big = jnp.int32(1 << 20)
    neg = jnp.float32(-jnp.inf)
    is_g = (lane >= N_EXPERTS) & (lane < N_EXPERTS + N_GROUPS)
    lg = jnp.where(is_g, logits, neg)
    pgu = jnp.exp(lg - jnp.max(lg, axis=1, keepdims=True))
    pg = pgu / jnp.sum(pgu, axis=1, keepdims=True)
    g_w = jnp.max(pg, axis=1, keepdims=True)
    g_idx = jnp.min(jnp.where(is_g & (pg == g_w), lane - N_EXPERTS, big), axis=1, keepdims=True)
    sel = (lane < N_EXPERTS) & ((lane // EXPERTS_PER_GROUP) == g_idx)
    le = jnp.where(sel, logits, neg)
    peu = jnp.exp(le - jnp.max(le, axis=1, keepdims=True))
    pe = peu / jnp.sum(peu, axis=1, keepdims=True)
    v1 = jnp.max(pe, axis=1, keepdims=True)
    i1 = jnp.min(jnp.where(sel & (pe == v1), lane, big), axis=1, keepdims=True)
    rest = sel & (lane != i1)
    pe2 = jnp.where(rest, pe, -1.0)
    v2 = jnp.max(pe2, axis=1, keepdims=True)
    i2 = jnp.min(jnp.where(rest & (pe2 == v2), lane, big), axis=1, keepdims=True)
    den = v1 + v2
    return (jnp.where(lane == i1, g_w * (v1 / den), 0.0)
            + jnp.where(lane == i2, g_w * (v2 / den), 0.0))


def _merge_kernel(o_ref, cx_ref, g0_ref, g1_ref, x_ref, m_ref, ng_ref, wap_ref, wcp_ref, wout_ref,
                  wrh_ref, wrl_ref, rb_ref, x1_ref, t_ref, comb_ref):
    ya = jnp.dot(o_ref[...], wap_ref[...], preferred_element_type=F32)
    yc = jnp.dot(cx_ref[...], wcp_ref[...], preferred_element_type=F32)
    merged = (jax.nn.sigmoid(g0_ref[...].astype(F32)) * ya
              + jax.nn.sigmoid(g1_ref[...].astype(F32)) * yc)
    mix = jnp.dot(merged.astype(BF16), wout_ref[...], preferred_element_type=F32)
    x1 = x_ref[...] + m_ref[0, 2:3, :] * mix
    x1_ref[...] = x1
    ms = jnp.mean(x1 * x1, axis=-1, keepdims=True)
    y = x1 * lax.rsqrt(ms + EPS) * ng_ref[...]
    t = y * (1.0 + m_ref[0, 4:5, :]) + m_ref[0, 3:4, :]
    t_ref[...] = t.astype(BF16)
    logits = _split_dot(t, wrh_ref[...], wrl_ref[...]) + rb_ref[...]
    comb_ref[...] = _route(logits)


def _merge(o2d, cx2d, px2d, x2d, m, ng, wap, wcp, wout, wr_hi, wr_lo, rb, *, rows_per_batch, tm):
    n = x2d.shape[0]
    tiles_per_batch = rows_per_batch // tm
    gblk = GATE_OFF // D_MODEL
    row = lambda i: (i, 0)
    const = lambda i: (0, 0)

    def resident(shape):
        return pl.BlockSpec(shape, const, pipeline_mode=pl.Buffered(1))

    return pl.pallas_call(
        _merge_kernel,
        grid=(n // tm,),
        in_specs=[pl.BlockSpec((tm, ATTN_WIDTH), row),
                  pl.BlockSpec((tm, CONV_CH), row),
                  pl.BlockSpec((tm, D_MODEL), lambda i: (i, gblk)),
                  pl.BlockSpec((tm, D_MODEL), lambda i: (i, gblk + 1)),
                  pl.BlockSpec((tm, D_MODEL), row),
                  pl.BlockSpec((1, N_MOD, D_MODEL), lambda i: (i // tiles_per_batch, 0, 0)),
                  pl.BlockSpec((1, D_MODEL), const),
                  resident((ATTN_WIDTH, D_MODEL)),
                  resident((CONV_CH, D_MODEL)),
                  resident((D_MODEL, D_MODEL)),
                  resident((D_MODEL, LANES)),
                  resident((D_MODEL, LANES)),
                  pl.BlockSpec((1, LANES), const)],
        out_specs=[pl.BlockSpec((tm, D_MODEL), row),
                   pl.BlockSpec((tm, D_MODEL), row),
                   pl.BlockSpec((tm, LANES), row)],
        out_shape=[jax.ShapeDtypeStruct((n, D_MODEL), F32),
                   jax.ShapeDtypeStruct((n, D_MODEL), BF16),
                   jax.ShapeDtypeStruct((n, LANES), F32)],
        compiler_params=_params("parallel"),
        name="merge",
    )(o2d, cx2d, px2d, px2d, x2d, m, ng, wap, wcp, wout, wr_hi, wr_lo, rb)


def _moe_kernel(t_ref, comb_ref, x1_ref, m_ref, fg_ref, w1_ref, w3_ref, w2_ref, o_ref, acc_ref):
    e = pl.program_id(1)

    @pl.when(e == 0)
    def _():
        acc_ref[...] = jnp.zeros(acc_ref.shape, F32)

    t = t_ref[...]
    h1 = jnp.dot(t, w1_ref[0], preferred_element_type=F32)
    h3 = jnp.dot(t, w3_ref[0], preferred_element_type=F32)
    hid = (h1 * jax.nn.sigmoid(h1)) * h3
    comb = comb_ref[...]
    lane = lax.broadcasted_iota(jnp.int32, comb.shape, 1)
    ce = jnp.sum(jnp.where(lane == e, comb, 0.0), axis=1, keepdims=True)
    acc_ref[...] += ce * jnp.dot(hid.astype(BF16), w2_ref[0], preferred_element_type=F32)

    @pl.when(e == N_EXPERTS - 1)
    def _():
        x2 = x1_ref[...] + m_ref[0, 5:6, :] * acc_ref[...]
        ms = jnp.mean(x2 * x2, axis=-1, keepdims=True)
        o_ref[...] = x2 * lax.rsqrt(ms + EPS) * fg_ref[...]


def _moe(t2d, comb, x1, m, fg, w1, w3, w2, *, rows_per_batch, tm):
    n = t2d.shape[0]
    tiles_per_batch = rows_per_batch // tm
    row = lambda i, e: (i, 0)
    return pl.pallas_call(
        _moe_kernel,
        grid=(n // tm, N_EXPERTS),
        in_specs=[pl.BlockSpec((tm, D_MODEL), row),
                  pl.BlockSpec((tm, LANES), row),
                  pl.BlockSpec((tm, D_MODEL), row),
                  pl.BlockSpec((1, N_MOD, D_MODEL), lambda i, e: (i // tiles_per_batch, 0, 0)),
                  pl.BlockSpec((1, D_MODEL), lambda i, e: (0, 0)),
                  pl.BlockSpec((1, D_MODEL, EXPERT_FF), lambda i, e: (e, 0, 0)),
                  pl.BlockSpec((1, D_MODEL, EXPERT_FF), lambda i, e: (e, 0, 0)),
                  pl.BlockSpec((1, EXPERT_FF, D_MODEL), lambda i, e: (e, 0, 0))],
        out_specs=pl.BlockSpec((tm, D_MODEL), row),
        out_shape=jax.ShapeDtypeStruct((n, D_MODEL), F32),
        scratch_shapes=[pltpu.VMEM((tm, D_MODEL), F32)],
        compiler_params=_params("parallel", "arbitrary"),
        name="moe",
    )(t2d, comb, x1, m, fg, w1, w3, w2)


def kernel(x, c, ctx, c_ctx, w_ada, b_ada, norm_attn_g, norm_ffn_g, w_in, lambda_q1, lambda_k1, lambda_q2, lambda_k2, attn_subln_g, conv_dw_w, conv_dw_b, conv_ln_g, conv_ln_b, w_attn_proj, w_conv_proj, w_out, router_group_w, router_group_b, router_expert_w, router_expert_b, expert_w1, expert_w3, expert_w2, final_norm_g):
    b, s, d = x.shape
    n_ctx = ctx.shape[1]
    depth = w_ada.shape[0]
    assert depth == 1 and d == D_MODEL and b <= 7
    i = 0
    lambda_init = 0.8 - 0.6 * math.exp(-0.3 * i)

    cond = jnp.zeros((8, d), F32).at[:b].set(c).at[b].set(c_ctx)
    mods = _ada(cond, w_ada[i], b_ada[i][None, :])
    m_x = mods[:b].reshape(b, N_MOD, d)
    m_c = mods[b:b + 1].reshape(1, N_MOD, d)

    w_in_bf = w_in[i].astype(BF16)
    tables = _rope_tables(s)
    g_attn = norm_attn_g[i][None, :]
    px = _inproj(x.reshape(b * s, d), m_x, g_attn, w_in_bf, tables,
                 rows_per_batch=s, seq=s, rope_cols=V_OFF, tm=1024, tn=1024)
    pc = _inproj(ctx.reshape(b * n_ctx, d), m_c, g_attn, w_in_bf[:, K_OFF:CONV_OFF], tables,
                 rows_per_batch=b * n_ctx, seq=s, rope_cols=0, tm=n_ctx, tn=1024)
    px3 = px.reshape(b, s, IN_COLS)
    pc3 = pc.reshape(b, n_ctx, CONV_OFF - K_OFF)

    lam_p = jnp.stack([lambda_q1[i], lambda_k1[i], lambda_q2[i], lambda_k2[i]])
    tk = 256
    skv = s + n_ctx
    qt = px3[..., Q_OFF:K_OFF].transpose(0, 2, 1)
    k_all = jnp.concatenate([px3[..., K_OFF:V_OFF], pc3[..., :V_OFF - K_OFF]], axis=1)
    v_all = jnp.concatenate([px3[..., V_OFF:CONV_OFF], pc3[..., V_OFF - K_OFF:]], axis=1)
    vt_all = v_all.reshape(b, skv // tk, tk, N_HEADS, V_DIM).transpose(0, 3, 1, 4, 2)
    o = _attention(qt, k_all, vt_all, lam_p, attn_subln_g[i][None, :], lambda_init=lambda_init, tq=256)
    cx = _conv(px3, conv_dw_w[i], conv_dw_b[i][None, :], conv_ln_g[i][None, :], conv_ln_b[i][None, :],
               ts=512)

    wr = jnp.zeros((d, LANES), F32)
    wr = wr.at[:, :N_EXPERTS].set(router_expert_w[i]).at[:, N_EXPERTS:N_EXPERTS + N_GROUPS].set(router_group_w[i])
    rb = jnp.zeros((1, LANES), F32)
    rb = rb.at[0, :N_EXPERTS].set(router_expert_b[i]).at[0, N_EXPERTS:N_EXPERTS + N_GROUPS].set(router_group_b[i])
    wr_hi = wr.astype(BF16)
    wr_lo = (wr - wr_hi.astype(F32)).astype(BF16)

    x1, t, comb = _merge(o.reshape(b * s, ATTN_WIDTH), cx.reshape(b * s, CONV_CH), px, x.reshape(b * s, d),
                         m_x, norm_ffn_g[i][None, :],
                         w_attn_proj[i].astype(BF16), w_conv_proj[i].astype(BF16), w_out[i].astype(BF16),
                         wr_hi, wr_lo, rb, rows_per_batch=s, tm=256)

    out = _moe(t, comb, x1, m_x, final_norm_g[None, :],
               expert_w1[i].astype(BF16), expert_w3[i].astype(BF16), expert_w2[i].astype(BF16),
               rows_per_batch=s, tm=512)
    return out.reshape(b, s, d)
```

```python
import functools
import math

import jax
import jax.numpy as jnp
from jax import lax
from jax.experimental import pallas as pl
from jax.experimental.pallas import tpu as pltpu

D_MODEL = 2048
GRID_W = 64
N_HEADS = 16
HEAD_DIM = 64
V_DIM = 2 * HEAD_DIM
ATTN_WIDTH = N_HEADS * V_DIM
CONV_CH = 1024
CONV_WIDTH = 31
N_GROUPS = 4
EXPERTS_PER_GROUP = 8
N_EXPERTS = N_GROUPS * EXPERTS_PER_GROUP
EXPERT_FF = 512
ROPE_BASE = 10000.0
ROPE_AXIS_DIM = HEAD_DIM // 2
N_MOD = 6
EPS = 1e-6

Q_OFF = 0
K_OFF = Q_OFF + N_HEADS * 2 * HEAD_DIM
V_OFF = K_OFF + N_HEADS * 2 * HEAD_DIM
CONV_OFF = V_OFF + ATTN_WIDTH
GATE_OFF = CONV_OFF + 2 * CONV_CH
IN_COLS = GATE_OFF + 2 * D_MODEL

LANES = 128
CONV_HALO = 16
VMEM_LIMIT = 56 * 1024 * 1024

F32 = jnp.float32
BF16 = jnp.bfloat16


def _params(*sem):
    return pltpu.CompilerParams(dimension_semantics=sem, vmem_limit_bytes=VMEM_LIMIT)


def _ada_kernel(c_ref, w_ref, b_ref, o_ref):
    c = c_ref[...]
    sc = c * jax.nn.sigmoid(c)
    o_ref[...] = jnp.dot(sc, w_ref[...], preferred_element_type=F32,
                         precision=lax.Precision.HIGHEST) + b_ref[...]


def _ada(cond, w, b):
    tn = 1024
    n = w.shape[1]
    return pl.pallas_call(
        _ada_kernel,
        grid=(n // tn,),
        in_specs=[pl.BlockSpec((8, D_MODEL), lambda j: (0, 0)),
                  pl.BlockSpec((D_MODEL, tn), lambda j: (0, j)),
                  pl.BlockSpec((1, tn), lambda j: (0, j))],
        out_specs=pl.BlockSpec((8, tn), lambda j: (0, j)),
        out_shape=jax.ShapeDtypeStruct((8, n), F32),
        compiler_params=_params("parallel"),
        name="ada",
    )(cond, w, b)


def _inproj_kernel(x_ref, m_ref, g_ref, w_ref, cos_ref, sa_ref, sb_ref, o_ref, hx_ref,
                   *, rope_tiles, q_tiles, tn):
    j = pl.program_id(1)

    @pl.when(j == 0)
    def _():
        x = x_ref[...]
        ms = jnp.mean(x * x, axis=-1, keepdims=True)
        y = x * lax.rsqrt(ms + EPS) * g_ref[...]
        shift = m_ref[0, 0:1, :]
        scale = m_ref[0, 1:2, :]
        hx_ref[...] = (y * (1.0 + scale) + shift).astype(BF16)

    acc = jnp.dot(hx_ref[...], w_ref[...], preferred_element_type=F32)

    if rope_tiles == 0:
        o_ref[...] = acc.astype(o_ref.dtype)
        return

    @pl.when(j < rope_tiles)
    def _():
        qs = jnp.where(j < q_tiles, HEAD_DIM ** -0.5 * math.log2(math.e), 1.0).astype(F32)
        cos = cos_ref[...] * qs
        sa = sa_ref[...] * qs
        sb = sb_ref[...] * qs
        for c in range(tn // LANES):
            a = acc[:, c * LANES:(c + 1) * LANES]
            r = (a * cos + pltpu.roll(a, LANES - ROPE_AXIS_DIM // 2, 1) * sa
                 + pltpu.roll(a, ROPE_AXIS_DIM // 2, 1) * sb)
            o_ref[:, c * LANES:(c + 1) * LANES] = r.astype(o_ref.dtype)

    @pl.when(j >= rope_tiles)
    def _():
        o_ref[...] = acc.astype(o_ref.dtype)


def _inproj(x2d, m, g, w, tables, *, rows_per_batch, seq, rope_cols, tm, tn):
    n, c = x2d.shape[0], w.shape[1]
    tiles_per_batch = rows_per_batch // tm
    seq_tiles = max(seq // tm, 1)
    cos, sa, sb = tables
    kern = functools.partial(_inproj_kernel, rope_tiles=rope_cols // tn,
                             q_tiles=(K_OFF - Q_OFF) // tn, tn=tn)
    tab_spec = pl.BlockSpec((tm, LANES), lambda i, j: (i % seq_tiles, 0))
    return pl.pallas_call(
        kern,
        grid=(n // tm, c // tn),
        in_specs=[pl.BlockSpec((tm, D_MODEL), lambda i, j: (i, 0)),
                  pl.BlockSpec((1, N_MOD, D_MODEL), lambda i, j: (i // tiles_per_batch, 0, 0)),
                  pl.BlockSpec((1, D_MODEL), lambda i, j: (0, 0)),
                  pl.BlockSpec((D_MODEL, tn), lambda i, j: (0, j)),
                  tab_spec, tab_spec, tab_spec],
        out_specs=pl.BlockSpec((tm, tn), lambda i, j: (i, j)),
        out_shape=jax.ShapeDtypeStruct((n, c), BF16),
        scratch_shapes=[pltpu.VMEM((tm, D_MODEL), BF16)],
        compiler_params=_params("parallel", "arbitrary"),
        name="inproj",
    )(x2d, m, g, w, cos, sa, sb)


def _rope_tables(seq):
    rows = seq // GRID_W
    row_pos = jnp.repeat(jnp.arange(rows), GRID_W).astype(F32)
    col_pos = jnp.tile(jnp.arange(GRID_W), rows).astype(F32)
    inv = ROPE_BASE ** (-jnp.arange(0, ROPE_AXIS_DIM, 2, dtype=F32) / ROPE_AXIS_DIM)
    ang_r = row_pos[:, None] * inv
    ang_c = col_pos[:, None] * inv
    ang = jnp.concatenate([ang_r, ang_r, ang_c, ang_c] * (LANES // HEAD_DIM), axis=-1)
    cos, sin = jnp.cos(ang), jnp.sin(ang)
    first_half = (jnp.arange(LANES) % ROPE_AXIS_DIM) < ROPE_AXIS_DIM // 2
    sa = jnp.where(first_half, -sin, 0.0)
    sb = jnp.where(first_half, 0.0, sin)
    return cos, sa, sb


SUM_ROWS = 16


def _attn_kernel(lam_ref, g_ref, qt_ref, k_ref, vt_ref, o_ref,
                 qm_ref, s_ref, p_ref, al_ref, m_ref, acc_ref, *, tq, tk, nt, unroll, lambda_init):
    qt = qt_ref[0]
    row = lax.broadcasted_iota(jnp.int32, qt.shape, 0)
    zero = jnp.zeros_like(qt)
    qm_ref[0] = jnp.where(row < HEAD_DIM, qt, zero)
    qm_ref[1] = jnp.where(row >= HEAD_DIM, qt, zero)
    m_ref[...] = jnp.full(m_ref.shape, -jnp.inf, F32)
    acc_ref[...] = jnp.zeros(acc_ref.shape, F32)
    al_ref[...] = jnp.ones(al_ref.shape, F32)
    p_ref[1] = jnp.zeros(p_ref.shape[1:], BF16)

    def scores(j, slot):
        off = pl.multiple_of(j * tk, tk)
        k = k_ref[0, pl.ds(off, tk), :]
        for mp in range(2):
            s_ref[slot, mp] = jnp.dot(k, qm_ref[mp], preferred_element_type=F32)

    def softmax(slot):
        for mp in range(2):
            s = s_ref[slot, mp]
            m_prev = m_ref[mp]
            m_new = jnp.maximum(m_prev, jnp.max(s, axis=0, keepdims=True))
            al_ref[slot, mp] = jnp.exp2(m_prev - m_new)
            m_ref[mp] = m_new
            p_ref[slot, mp] = jnp.exp2((s - m_new).astype(BF16))

    def pv(j, slot):
        vt = vt_ref[0, 0, j]
        for mp in range(2):
            acc_ref[mp] = (al_ref[slot, mp] * acc_ref[mp]
                           + jnp.dot(vt, p_ref[slot, mp], preferred_element_type=F32))

    scores(0, 0)

    def step(j, par, with_scores=True):
        if with_scores:
            scores(jnp.minimum(j + 1, nt - 1), 1 - par)
        softmax(par)
        pv(jnp.maximum(j - 1, 0), 1 - par)

    def body(t, carry):
        for u in range(unroll):
            step(unroll * t + u, u % 2)
        return carry

    lax.fori_loop(0, nt // unroll, body, 0)
    for j in range(nt - nt % unroll, nt):
        step(j, j % 2, with_scores=j + 1 < nt)
    pv(nt - 1, (nt - 1) % 2)

    lam_p = lam_ref[...]
    lam = (jnp.exp(jnp.sum(lam_p[0:1] * lam_p[1:2], axis=1, keepdims=True))
           - jnp.exp(jnp.sum(lam_p[2:3] * lam_p[3:4], axis=1, keepdims=True)) + lambda_init)
    a0, a1 = acc_ref[0], acc_ref[1]
    ot = a0[:V_DIM] / a0[V_DIM:V_DIM + 1] - lam * (a1[:V_DIM] / a1[V_DIM:V_DIM + 1])
    ms = jnp.mean(ot * ot, axis=0, keepdims=True)
    ot = ot * lax.rsqrt(ms + EPS)
    o = ot.T * (g_ref[...] * (1.0 - lambda_init))
    o_ref[0] = o.astype(o_ref.dtype)


def _attention(qt, k_all, vt_all, lam_p, g, *, lambda_init, tq, unroll):
    b, _, s = qt.shape
    skv = k_all.shape[1]
    nt, vrows, tk = vt_all.shape[2:]
    assert unroll % 2 == 0
    kern = functools.partial(_attn_kernel, tq=tq, tk=tk, nt=nt, unroll=unroll, lambda_init=lambda_init)
    return pl.pallas_call(
        kern,
        grid=(b, N_HEADS, s // tq),
        in_specs=[pl.BlockSpec((4, HEAD_DIM), lambda bb, h, i: (0, 0)),
                  pl.BlockSpec((1, V_DIM), lambda bb, h, i: (0, 0)),
                  pl.BlockSpec((1, LANES, tq), lambda bb, h, i: (bb, h, i)),
                  pl.BlockSpec((1, skv, LANES), lambda bb, h, i: (bb, 0, h)),
                  pl.BlockSpec((1, 1, nt, vrows, tk), lambda bb, h, i: (bb, h, 0, 0, 0))],
        out_specs=pl.BlockSpec((1, tq, LANES), lambda bb, h, i: (bb, i, h)),
        out_shape=jax.ShapeDtypeStruct((b, s, ATTN_WIDTH), BF16),
        scratch_shapes=[pltpu.VMEM((2, LANES, tq), BF16),
                        pltpu.VMEM((2, 2, tk, tq), F32),
                        pltpu.VMEM((2, 2, tk, tq), BF16),
                        pltpu.VMEM((2, 2, 1, tq), F32),
                        pltpu.VMEM((2, 1, tq), F32),
                        pltpu.VMEM((2, vrows, tq), F32)],
        compiler_params=_params("parallel", "parallel", "arbitrary"),
        name="attn",
    )(lam_p, g, qt, k_all, vt_all)


def _conv_kernel(uc_ref, up_ref, un_ref, w_ref, b_ref, g_ref, be_ref, o_ref, h_ref, *, ts, n_tiles):
    i = pl.program_id(1)

    def glu(u):
        u = u.astype(F32)
        return u[:, :CONV_CH] * jax.nn.sigmoid(u[:, CONV_CH:])

    h_ref[0:CONV_HALO, :] = glu(up_ref[0]) * jnp.where(i > 0, 1.0, 0.0).astype(F32)
    h_ref[CONV_HALO:CONV_HALO + ts, :] = glu(uc_ref[0])
    h_ref[CONV_HALO + ts:, :] = glu(un_ref[0]) * jnp.where(i < n_tiles - 1, 1.0, 0.0).astype(F32)

    w = w_ref[...]
    base = CONV_HALO - CONV_WIDTH // 2
    acc = jnp.zeros((ts, CONV_CH), F32) + b_ref[...]
    for j in range(CONV_WIDTH):
        acc = acc + h_ref[base + j:base + j + ts, :] * w[j:j + 1, :]
    mu = jnp.mean(acc, axis=-1, keepdims=True)
    d = acc - mu
    var = jnp.mean(d * d, axis=-1, keepdims=True)
    y = d * lax.rsqrt(var + EPS) * g_ref[...] + be_ref[...]
    o_ref[0] = (y * jax.nn.sigmoid(y)).astype(o_ref.dtype)


def _conv(px, w, b, g, be, *, ts):
    bsz, s, _ = px.shape
    n_tiles = s // ts
    ublk = CONV_OFF // (2 * CONV_CH)
    hpt = ts // CONV_HALO
    n_halo = s // CONV_HALO
    kern = functools.partial(_conv_kernel, ts=ts, n_tiles=n_tiles)
    vec = pl.BlockSpec((1, CONV_CH), lambda bb, i: (0, 0))
    return pl.pallas_call(
        kern,
        grid=(bsz, n_tiles),
        in_specs=[pl.BlockSpec((1, ts, 2 * CONV_CH), lambda bb, i: (bb, i, ublk)),
                  pl.BlockSpec((1, CONV_HALO, 2 * CONV_CH),
                               lambda bb, i: (bb, jnp.maximum(i * hpt - 1, 0), ublk)),
                  pl.BlockSpec((1, CONV_HALO, 2 * CONV_CH),
                               lambda bb, i: (bb, jnp.minimum((i + 1) * hpt, n_halo - 1), ublk)),
                  pl.BlockSpec((CONV_WIDTH, CONV_CH), lambda bb, i: (0, 0)),
                  vec, vec, vec],
        out_specs=pl.BlockSpec((1, ts, CONV_CH), lambda bb, i: (bb, i, 0)),
        out_shape=jax.ShapeDtypeStruct((bsz, s, CONV_CH), BF16),
        scratch_shapes=[pltpu.VMEM((ts + 2 * CONV_HALO, CONV_CH), F32)],
        compiler_params=_params("parallel", "parallel"),
        name="conv",
    )(px, px, px, w, b, g, be)


def _split_dot(t, w_hi, w_lo):
    t_hi = t.astype(BF16)
    t_lo = (t - t_hi.astype(F32)).astype(BF16)
    return (jnp.dot(t_hi, w_hi, preferred_element_type=F32)
            + jnp.dot(t_hi, w_lo, preferred_element_type=F32)
            + jnp.dot(t_lo, w_hi, preferred_element_type=F32))


def _route(logits):
    lane = lax.broadcasted_iota(jnp.int32, logits.shape, 1)
    big = jnp.int32(1 << 20)
    neg = jnp.float32(-jnp.inf)
    is_g = (lane >= N_EXPERTS) & (lane < N_EXPERTS + N_GROUPS)
    lg = jnp.where(is_g, logits, neg)
    pgu = jnp.exp(lg - jnp.max(lg, axis=1, keepdims=True))
    pg = pgu / jnp.sum(pgu, axis=1, keepdims=True)
    g_w = jnp.max(pg, axis=1, keepdims=True)
    g_idx = jnp.min(jnp.where(is_g & (pg == g_w), lane - N_EXPERTS, big), axis=1, keepdims=True)
    sel = (lane < N_EXPERTS) & ((lane // EXPERTS_PER_GROUP) == g_idx)
    le = jnp.where(sel, logits, neg)
    peu = jnp.exp(le - jnp.max(le, axis=1, keepdims=True))
    pe = peu / jnp.sum(peu, axis=1, keepdims=True)
    v1 = jnp.max(pe, axis=1, keepdims=True)
    i1 = jnp.min(jnp.where(sel & (pe == v1), lane, big), axis=1, keepdims=True)
    rest = sel & (lane != i1)
    pe2 = jnp.where(rest, pe, -1.0)
    v2 = jnp.max(pe2, axis=1, keepdims=True)
    i2 = jnp.min(jnp.where(rest & (pe2 == v2), lane, big), axis=1, keepdims=True)
    den = v1 + v2
    return (jnp.where(lane == i1, g_w * (v1 / den), 0.0)
            + jnp.where(lane == i2, g_w * (v2 / den), 0.0))


def _merge_kernel(o_ref, cx_ref, g0_ref, g1_ref, x_ref, m_ref, ng_ref, wap_ref, wcp_ref, wout_ref,
                  wrh_ref, wrl_ref, rb_ref, x1_ref, t_ref, comb_ref):
    ya = jnp.dot(o_ref[...], wap_ref[...], preferred_element_type=F32)
    yc = jnp.dot(cx_ref[...], wcp_ref[...], preferred_element_type=F32)
    merged = (jax.nn.sigmoid(g0_ref[...].astype(F32)) * ya
              + jax.nn.sigmoid(g1_ref[...].astype(F32)) * yc)
    mix = jnp.dot(merged.astype(BF16), wout_ref[...], preferred_element_type=F32)
    x1 = x_ref[...] + m_ref[0, 2:3, :] * mix
    x1_ref[...] = x1
    ms = jnp.mean(x1 * x1, axis=-1, keepdims=True)
    y = x1 * lax.rsqrt(ms + EPS) * ng_ref[...]
    t = y * (1.0 + m_ref[0, 4:5, :]) + m_ref[0, 3:4, :]
    t_ref[...] = t.astype(BF16)
    logits = _split_dot(t, wrh_ref[...], wrl_ref[...]) + rb_ref[...]
    comb_ref[...] = _route(logits)


def _merge(o2d, cx2d, px2d, x2d, m, ng, wap, wcp, wout, wr_hi, wr_lo, rb, *, rows_per_batch, tm):
    n = x2d.shape[0]
    tiles_per_batch = rows_per_batch // tm
    gblk = GATE_OFF // D_MODEL
    row = lambda i: (i, 0)
    const = lambda i: (0, 0)

    def resident(shape):
        return pl.BlockSpec(shape, const, pipeline_mode=pl.Buffered(1))

    return pl.pallas_call(
        _merge_kernel,
        grid=(n // tm,),
        in_specs=[pl.BlockSpec((tm, ATTN_WIDTH), row),
                  pl.BlockSpec((tm, CONV_CH), row),
                  pl.BlockSpec((tm, D_MODEL), lambda i: (i, gblk)),
                  pl.BlockSpec((tm, D_MODEL), lambda i: (i, gblk + 1)),
                  pl.BlockSpec((tm, D_MODEL), row),
                  pl.BlockSpec((1, N_MOD, D_MODEL), lambda i: (i // tiles_per_batch, 0, 0)),
                  pl.BlockSpec((1, D_MODEL), const),
                  resident((ATTN_WIDTH, D_MODEL)),
                  resident((CONV_CH, D_MODEL)),
                  resident((D_MODEL, D_MODEL)),
                  resident((D_MODEL, LANES)),
                  resident((D_MODEL, LANES)),
                  pl.BlockSpec((1, LANES), const)],
        out_specs=[pl.BlockSpec((tm, D_MODEL), row),
                   pl.BlockSpec((tm, D_MODEL), row),
                   pl.BlockSpec((tm, LANES), row)],
        out_shape=[jax.ShapeDtypeStruct((n, D_MODEL), F32),
                   jax.ShapeDtypeStruct((n, D_MODEL), BF16),
                   jax.ShapeDtypeStruct((n, LANES), F32)],
        compiler_params=_params("parallel"),
        name="merge",
    )(o2d, cx2d, px2d, px2d, x2d, m, ng, wap, wcp, wout, wr_hi, wr_lo, rb)


def _moe_kernel(t_ref, comb_ref, x1_ref, m_ref, fg_ref, w1_ref, w3_ref, w2_ref, o_ref, acc_ref):
    e = pl.program_id(1)

    @pl.when(e == 0)
    def _():
        acc_ref[...] = jnp.zeros(acc_ref.shape, F32)

    t = t_ref[...]
    h1 = jnp.dot(t, w1_ref[0], preferred_element_type=F32)
    h3 = jnp.dot(t, w3_ref[0], preferred_element_type=F32)
    hid = (h1 * jax.nn.sigmoid(h1)) * h3
    comb = comb_ref[...]
    lane = lax.broadcasted_iota(jnp.int32, comb.shape, 1)
    ce = jnp.sum(jnp.where(lane == e, comb, 0.0), axis=1, keepdims=True)
    acc_ref[...] += ce * jnp.dot(hid.astype(BF16), w2_ref[0], preferred_element_type=F32)

    @pl.when(e == N_EXPERTS - 1)
    def _():
        x2 = x1_ref[...] + m_ref[0, 5:6, :] * acc_ref[...]
        ms = jnp.mean(x2 * x2, axis=-1, keepdims=True)
        o_ref[...] = x2 * lax.rsqrt(ms + EPS) * fg_ref[...]


def _moe(t2d, comb, x1, m, fg, w1, w3, w2, *, rows_per_batch, tm):
    n = t2d.shape[0]
    tiles_per_batch = rows_per_batch // tm
    row = lambda i, e: (i, 0)
    return pl.pallas_call(
        _moe_kernel,
        grid=(n // tm, N_EXPERTS),
        in_specs=[pl.BlockSpec((tm, D_MODEL), row),
                  pl.BlockSpec((tm, LANES), row),
                  pl.BlockSpec((tm, D_MODEL), row),
                  pl.BlockSpec((1, N_MOD, D_MODEL), lambda i, e: (i // tiles_per_batch, 0, 0)),
                  pl.BlockSpec((1, D_MODEL), lambda i, e: (0, 0)),
                  pl.BlockSpec((1, D_MODEL, EXPERT_FF), lambda i, e: (e, 0, 0)),
                  pl.BlockSpec((1, D_MODEL, EXPERT_FF), lambda i, e: (e, 0, 0)),
                  pl.BlockSpec((1, EXPERT_FF, D_MODEL), lambda i, e: (e, 0, 0))],
        out_specs=pl.BlockSpec((tm, D_MODEL), row),
        out_shape=jax.ShapeDtypeStruct((n, D_MODEL), F32),
        scratch_shapes=[pltpu.VMEM((tm, D_MODEL), F32)],
        compiler_params=_params("parallel", "arbitrary"),
        name="moe",
    )(t2d, comb, x1, m, fg, w1, w3, w2)


def kernel(x, c, ctx, c_ctx, w_ada, b_ada, norm_attn_g, norm_ffn_g, w_in, lambda_q1, lambda_k1, lambda_q2, lambda_k2, attn_subln_g, conv_dw_w, conv_dw_b, conv_ln_g, conv_ln_b, w_attn_proj, w_conv_proj, w_out, router_group_w, router_group_b, router_expert_w, router_expert_b, expert_w1, expert_w3, expert_w2, final_norm_g):
    b, s, d = x.shape
    n_ctx = ctx.shape[1]
    depth = w_ada.shape[0]
    assert depth == 1 and d == D_MODEL and b <= 7
    i = 0
    lambda_init = 0.8 - 0.6 * math.exp(-0.3 * i)

    cond = jnp.zeros((8, d), F32).at[:b].set(c).at[b].set(c_ctx)
    mods = _ada(cond, w_ada[i], b_ada[i][None, :])
    m_x = mods[:b].reshape(b, N_MOD, d)
    m_c = mods[b:b + 1].reshape(1, N_MOD, d)

    w_in_bf = w_in[i].astype(BF16)
    tables = _rope_tables(s)
    g_attn = norm_attn_g[i][None, :]
    px = _inproj(x.reshape(b * s, d), m_x, g_attn, w_in_bf, tables,
                 rows_per_batch=s, seq=s, rope_cols=V_OFF, tm=1024, tn=1024)
    pc = _inproj(ctx.reshape(b * n_ctx, d), m_c, g_attn, w_in_bf[:, K_OFF:CONV_OFF], tables,
                 rows_per_batch=b * n_ctx, seq=s, rope_cols=0, tm=n_ctx, tn=1024)
    px3 = px.reshape(b, s, IN_COLS)
    pc3 = pc.reshape(b, n_ctx, CONV_OFF - K_OFF)

    lam_p = jnp.stack([lambda_q1[i], lambda_k1[i], lambda_q2[i], lambda_k2[i]])
    tk = 256
    skv = s + n_ctx
    qt = px3[..., Q_OFF:K_OFF].transpose(0, 2, 1)
    k_all = jnp.concatenate([px3[..., K_OFF:V_OFF], pc3[..., :V_OFF - K_OFF]], axis=1)
    v_all = jnp.concatenate([px3[..., V_OFF:CONV_OFF], pc3[..., V_OFF - K_OFF:]], axis=1)
    vt_all = v_all.reshape(b, skv // tk, tk, N_HEADS, V_DIM).transpose(0, 3, 1, 4, 2)
    vt_all = jnp.concatenate([vt_all, jnp.ones(vt_all.shape[:3] + (SUM_ROWS, tk), BF16)], axis=3)
    o = _attention(qt, k_all, vt_all, lam_p, attn_subln_g[i][None, :], lambda_init=lambda_init, tq=256,
                   unroll=4)
    cx = _conv(px3, conv_dw_w[i], conv_dw_b[i][None, :], conv_ln_g[i][None, :], conv_ln_b[i][None, :],
               ts=512)

    wr = jnp.zeros((d, LANES), F32)
    wr = wr.at[:, :N_EXPERTS].set(router_expert_w[i]).at[:, N_EXPERTS:N_EXPERTS + N_GROUPS].set(router_group_w[i])
    rb = jnp.zeros((1, LANES), F32)
    rb = rb.at[0, :N_EXPERTS].set(router_expert_b[i]).at[0, N_EXPERTS:N_EXPERTS + N_GROUPS].set(router_group_b[i])
    wr_hi = wr.astype(BF16)
    wr_lo = (wr - wr_hi.astype(F32)).astype(BF16)

    x1, t, comb = _merge(o.reshape(b * s, ATTN_WIDTH), cx.reshape(b * s, CONV_CH), px, x.reshape(b * s, d),
                         m_x, norm_ffn_g[i][None, :],
                         w_attn_proj[i].astype(BF16), w_conv_proj[i].astype(BF16), w_out[i].astype(BF16),
                         wr_hi, wr_lo, rb, rows_per_batch=s, tm=256)

    out = _moe(t, comb, x1, m_x, final_norm_g[None, :],
               expert_w1[i].astype(BF16), expert_w3[i].astype(BF16), expert_w2[i].astype(BF16),
               rows_per_batch=s, tm=512)
    return out.reshape(b, s, d)
```

```python
import functools
import math

import jax
import jax.numpy as jnp
from jax import lax
from jax.experimental import pallas as pl
from jax.experimental.pallas import tpu as pltpu

D_MODEL = 2048
GRID_W = 64
N_HEADS = 16
HEAD_DIM = 64
V_DIM = 2 * HEAD_DIM
ATTN_WIDTH = N_HEADS * V_DIM
CONV_CH = 1024
CONV_WIDTH = 31
N_GROUPS = 4
EXPERTS_PER_GROUP = 8
N_EXPERTS = N_GROUPS * EXPERTS_PER_GROUP
EXPERT_FF = 512
ROPE_BASE = 10000.0
ROPE_AXIS_DIM = HEAD_DIM // 2
N_MOD = 6
EPS = 1e-6

Q_OFF = 0
K_OFF = Q_OFF + N_HEADS * 2 * HEAD_DIM
V_OFF = K_OFF + N_HEADS * 2 * HEAD_DIM
CONV_OFF = V_OFF + ATTN_WIDTH
GATE_OFF = CONV_OFF + 2 * CONV_CH
IN_COLS = GATE_OFF + 2 * D_MODEL

LANES = 128
CONV_HALO = 16
VMEM_LIMIT = 56 * 1024 * 1024

F32 = jnp.float32
BF16 = jnp.bfloat16


def _params(*sem):
    return pltpu.CompilerParams(dimension_semantics=sem, vmem_limit_bytes=VMEM_LIMIT)


def _ada_kernel(c_ref, w_ref, b_ref, o_ref):
    c = c_ref[...]
    sc = c * jax.nn.sigmoid(c)
    o_ref[...] = jnp.dot(sc, w_ref[...], preferred_element_type=F32,
                         precision=lax.Precision.HIGHEST) + b_ref[...]


def _ada(cond, w, b):
    tn = 1024
    n = w.shape[1]
    return pl.pallas_call(
        _ada_kernel,
        grid=(n // tn,),
        in_specs=[pl.BlockSpec((8, D_MODEL), lambda j: (0, 0)),
                  pl.BlockSpec((D_MODEL, tn), lambda j: (0, j)),
                  pl.BlockSpec((1, tn), lambda j: (0, j))],
        out_specs=pl.BlockSpec((8, tn), lambda j: (0, j)),
        out_shape=jax.ShapeDtypeStruct((8, n), F32),
        compiler_params=_params("parallel"),
        name="ada",
    )(cond, w, b)


def _inproj_kernel(x_ref, m_ref, g_ref, w_ref, cos_ref, sa_ref, sb_ref, o_ref, hx_ref,
                   *, rope_tiles, q_tiles, tn):
    j = pl.program_id(1)

    @pl.when(j == 0)
    def _():
        x = x_ref[...]
        ms = jnp.mean(x * x, axis=-1, keepdims=True)
        y = x * lax.rsqrt(ms + EPS) * g_ref[...]
        shift = m_ref[0, 0:1, :]
        scale = m_ref[0, 1:2, :]
        hx_ref[...] = (y * (1.0 + scale) + shift).astype(BF16)

    acc = jnp.dot(hx_ref[...], w_ref[...], preferred_element_type=F32)

    if rope_tiles == 0:
        o_ref[...] = acc.astype(o_ref.dtype)
        return

    @pl.when(j < rope_tiles)
    def _():
        qs = jnp.where(j < q_tiles, HEAD_DIM ** -0.5 * math.log2(math.e), 1.0).astype(F32)
        cos = cos_ref[...] * qs
        sa = sa_ref[...] * qs
        sb = sb_ref[...] * qs
        for c in range(tn // LANES):
            a = acc[:, c * LANES:(c + 1) * LANES]
            r = (a * cos + pltpu.roll(a, LANES - ROPE_AXIS_DIM // 2, 1) * sa
                 + pltpu.roll(a, ROPE_AXIS_DIM // 2, 1) * sb)
            o_ref[:, c * LANES:(c + 1) * LANES] = r.astype(o_ref.dtype)

    @pl.when(j >= rope_tiles)
    def _():
        o_ref[...] = acc.astype(o_ref.dtype)


def _inproj(x2d, m, g, w, tables, *, rows_per_batch, seq, rope_cols, tm, tn):
    n, c = x2d.shape[0], w.shape[1]
    tiles_per_batch = rows_per_batch // tm
    seq_tiles = max(seq // tm, 1)
    cos, sa, sb = tables
    kern = functools.partial(_inproj_kernel, rope_tiles=rope_cols // tn,
                             q_tiles=(K_OFF - Q_OFF) // tn, tn=tn)
    tab_spec = pl.BlockSpec((tm, LANES), lambda i, j: (i % seq_tiles, 0))
    return pl.pallas_call(
        kern,
        grid=(n // tm, c // tn),
        in_specs=[pl.BlockSpec((tm, D_MODEL), lambda i, j: (i, 0)),
                  pl.BlockSpec((1, N_MOD, D_MODEL), lambda i, j: (i // tiles_per_batch, 0, 0)),
                  pl.BlockSpec((1, D_MODEL), lambda i, j: (0, 0)),
                  pl.BlockSpec((D_MODEL, tn), lambda i, j: (0, j)),
                  tab_spec, tab_spec, tab_spec],
        out_specs=pl.BlockSpec((tm, tn), lambda i, j: (i, j)),
        out_shape=jax.ShapeDtypeStruct((n, c), BF16),
        scratch_shapes=[pltpu.VMEM((tm, D_MODEL), BF16)],
        compiler_params=_params("parallel", "arbitrary"),
        name="inproj",
    )(x2d, m, g, w, cos, sa, sb)


def _rope_tables(seq):
    rows = seq // GRID_W
    row_pos = jnp.repeat(jnp.arange(rows), GRID_W).astype(F32)
    col_pos = jnp.tile(jnp.arange(GRID_W), rows).astype(F32)
    inv = ROPE_BASE ** (-jnp.arange(0, ROPE_AXIS_DIM, 2, dtype=F32) / ROPE_AXIS_DIM)
    ang_r = row_pos[:, None] * inv
    ang_c = col_pos[:, None] * inv
    ang = jnp.concatenate([ang_r, ang_r, ang_c, ang_c] * (LANES // HEAD_DIM), axis=-1)
    cos, sin = jnp.cos(ang), jnp.sin(ang)
    first_half = (jnp.arange(LANES) % ROPE_AXIS_DIM) < ROPE_AXIS_DIM // 2
    sa = jnp.where(first_half, -sin, 0.0)
    sb = jnp.where(first_half, 0.0, sin)
    return cos, sa, sb


SUM_ROWS = 16


def _attn_kernel(lam_ref, g_ref, qt_ref, k_ref, vt_ref, o_ref,
                 qm_ref, s_ref, p_ref, al_ref, m_ref, acc_ref, *, tq, tk, nt, unroll, lambda_init):
    qt = qt_ref[0]
    row = lax.broadcasted_iota(jnp.int32, qt.shape, 0)
    zero = jnp.zeros_like(qt)
    qm_ref[0] = jnp.where(row < HEAD_DIM, qt, zero)
    qm_ref[1] = jnp.where(row >= HEAD_DIM, qt, zero)
    m_ref[...] = jnp.full(m_ref.shape, -jnp.inf, F32)
    acc_ref[...] = jnp.zeros(acc_ref.shape, F32)
    al_ref[...] = jnp.ones(al_ref.shape, F32)
    p_ref[1] = jnp.zeros(p_ref.shape[1:], BF16)

    def scores(j, slot):
        off = pl.multiple_of(j * tk, tk)
        k = k_ref[0, pl.ds(off, tk), :]
        for mp in range(2):
            s_ref[slot, mp] = jnp.dot(k, qm_ref[mp], preferred_element_type=F32)

    def softmax(slot):
        for mp in range(2):
            s = s_ref[slot, mp]
            m_prev = m_ref[mp]
            m_new = jnp.maximum(m_prev, jnp.max(s, axis=0, keepdims=True))
            al_ref[slot, mp] = jnp.exp2(m_prev - m_new)
            m_ref[mp] = m_new
            p_ref[slot, mp] = jnp.exp2((s - m_new).astype(BF16))

    def pv(j, slot):
        vt = vt_ref[0, 0, j]
        for mp in range(2):
            acc_ref[mp] = (al_ref[slot, mp] * acc_ref[mp]
                           + jnp.dot(vt, p_ref[slot, mp], preferred_element_type=F32))

    scores(0, 0)

    def step(j, par, with_scores=True):
        if with_scores:
            scores(jnp.minimum(j + 1, nt - 1), 1 - par)
        softmax(par)
        pv(jnp.maximum(j - 1, 0), 1 - par)

    def body(t, carry):
        for u in range(unroll):
            step(unroll * t + u, u % 2)
        return carry

    lax.fori_loop(0, nt // unroll, body, 0)
    for j in range(nt - nt % unroll, nt):
        step(j, j % 2, with_scores=j + 1 < nt)
    pv(nt - 1, (nt - 1) % 2)

    lam_p = lam_ref[...]
    lam = (jnp.exp(jnp.sum(lam_p[0:1] * lam_p[1:2], axis=1, keepdims=True))
           - jnp.exp(jnp.sum(lam_p[2:3] * lam_p[3:4], axis=1, keepdims=True)) + lambda_init)
    a0, a1 = acc_ref[0], acc_ref[1]
    ot = a0[:V_DIM] / a0[V_DIM:V_DIM + 1] - lam * (a1[:V_DIM] / a1[V_DIM:V_DIM + 1])
    ms = jnp.mean(ot * ot, axis=0, keepdims=True)
    ot = ot * lax.rsqrt(ms + EPS)
    o = ot.T * (g_ref[...] * (1.0 - lambda_init))
    o_ref[0] = o.astype(o_ref.dtype)


def _attention(qt, k_all, vt_all, lam_p, g, *, lambda_init, tq, unroll):
    b, _, s = qt.shape
    skv = k_all.shape[1]
    nt, vrows, tk = vt_all.shape[2:]
    assert unroll % 2 == 0
    kern = functools.partial(_attn_kernel, tq=tq, tk=tk, nt=nt, unroll=unroll, lambda_init=lambda_init)
    return pl.pallas_call(
        kern,
        grid=(b, N_HEADS, s // tq),
        in_specs=[pl.BlockSpec((4, HEAD_DIM), lambda bb, h, i: (0, 0)),
                  pl.BlockSpec((1, V_DIM), lambda bb, h, i: (0, 0)),
                  pl.BlockSpec((1, LANES, tq), lambda bb, h, i: (bb, h, i)),
                  pl.BlockSpec((1, skv, LANES), lambda bb, h, i: (bb, 0, h)),
                  pl.BlockSpec((1, 1, nt, vrows, tk), lambda bb, h, i: (bb, h, 0, 0, 0))],
        out_specs=pl.BlockSpec((1, tq, LANES), lambda bb, h, i: (bb, i, h)),
        out_shape=jax.ShapeDtypeStruct((b, s, ATTN_WIDTH), BF16),
        scratch_shapes=[pltpu.VMEM((2, LANES, tq), BF16),
                        pltpu.VMEM((2, 2, tk, tq), F32),
                        pltpu.VMEM((2, 2, tk, tq), BF16),
                        pltpu.VMEM((2, 2, 1, tq), F32),
                        pltpu.VMEM((2, 1, tq), F32),
                        pltpu.VMEM((2, vrows, tq), F32)],
        compiler_params=_params("parallel", "parallel", "arbitrary"),
        name="attn",
    )(lam_p, g, qt, k_all, vt_all)


def _conv_kernel(uc_ref, up_ref, un_ref, w_ref, b_ref, g_ref, be_ref, o_ref, h_ref, *, ts, n_tiles):
    i = pl.program_id(1)

    def glu(u):
        u = u.astype(F32)
        return u[:, :CONV_CH] * jax.nn.sigmoid(u[:, CONV_CH:])

    h_ref[0:CONV_HALO, :] = glu(up_ref[0]) * jnp.where(i > 0, 1.0, 0.0).astype(F32)
    h_ref[CONV_HALO:CONV_HALO + ts, :] = glu(uc_ref[0])
    h_ref[CONV_HALO + ts:, :] = glu(un_ref[0]) * jnp.where(i < n_tiles - 1, 1.0, 0.0).astype(F32)

    w = w_ref[...]
    base = CONV_HALO - CONV_WIDTH // 2
    acc = jnp.zeros((ts, CONV_CH), F32) + b_ref[...]
    for j in range(CONV_WIDTH):
        acc = acc + h_ref[base + j:base + j + ts, :] * w[j:j + 1, :]
    mu = jnp.mean(acc, axis=-1, keepdims=True)
    d = acc - mu
    var = jnp.mean(d * d, axis=-1, keepdims=True)
    y = d * lax.rsqrt(var + EPS) * g_ref[...] + be_ref[...]
    o_ref[0] = (y * jax.nn.sigmoid(y)).astype(o_ref.dtype)


def _conv(px, w, b, g, be, *, ts):
    bsz, s, _ = px.shape
    n_tiles = s // ts
    ublk = CONV_OFF // (2 * CONV_CH)
    hpt = ts // CONV_HALO
    n_halo = s // CONV_HALO
    kern = functools.partial(_conv_kernel, ts=ts, n_tiles=n_tiles)
    vec = pl.BlockSpec((1, CONV_CH), lambda bb, i: (0, 0))
    return pl.pallas_call(
        kern,
        grid=(bsz, n_tiles),
        in_specs=[pl.BlockSpec((1, ts, 2 * CONV_CH), lambda bb, i: (bb, i, ublk)),
                  pl.BlockSpec((1, CONV_HALO, 2 * CONV_CH),
                               lambda bb, i: (bb, jnp.maximum(i * hpt - 1, 0), ublk)),
                  pl.BlockSpec((1, CONV_HALO, 2 * CONV_CH),
                               lambda bb, i: (bb, jnp.minimum((i + 1) * hpt, n_halo - 1), ublk)),
                  pl.BlockSpec((CONV_WIDTH, CONV_CH), lambda bb, i: (0, 0)),
                  vec, vec, vec],
        out_specs=pl.BlockSpec((1, ts, CONV_CH), lambda bb, i: (bb, i, 0)),
        out_shape=jax.ShapeDtypeStruct((bsz, s, CONV_CH), BF16),
        scratch_shapes=[pltpu.VMEM((ts + 2 * CONV_HALO, CONV_CH), F32)],
        compiler_params=_params("parallel", "parallel"),
        name="conv",
    )(px, px, px, w, b, g, be)


def _split_dot(t, w_hi, w_lo):
    t_hi = t.astype(BF16)
    t_lo = (t - t_hi.astype(F32)).astype(BF16)
    return (jnp.dot(t_hi, w_hi, preferred_element_type=F32)
            + jnp.dot(t_hi, w_lo, preferred_element_type=F32)
            + jnp.dot(t_lo, w_hi, preferred_element_type=F32))


def _route(logits):
    lane = lax.broadcasted_iota(jnp.int32, logits.shape, 1)
    big = jnp.int32(1 << 20)
    neg = jnp.float32(-jnp.inf)
    is_g = (lane >= N_EXPERTS) & (lane < N_EXPERTS + N_GROUPS)
    lg = jnp.where(is_g, logits, neg)
    pgu = jnp.exp(lg - jnp.max(lg, axis=1, keepdims=True))
    pg = pgu / jnp.sum(pgu, axis=1, keepdims=True)
    g_w = jnp.max(pg, axis=1, keepdims=True)
    g_idx = jnp.min(jnp.where(is_g & (pg == g_w), lane - N_EXPERTS, big), axis=1, keepdims=True)
    sel = (lane < N_EXPERTS) & ((lane // EXPERTS_PER_GROUP) == g_idx)
    le = jnp.where(sel, logits, neg)
    peu = jnp.exp(le - jnp.max(le, axis=1, keepdims=True))
    pe = peu / jnp.sum(peu, axis=1, keepdims=True)
    v1 = jnp.max(pe, axis=1, keepdims=True)
    i1 = jnp.min(jnp.where(sel & (pe == v1), lane, big), axis=1, keepdims=True)
    rest = sel & (lane != i1)
    pe2 = jnp.where(rest, pe, -1.0)
    v2 = jnp.max(pe2, axis=1, keepdims=True)
    i2 = jnp.min(jnp.where(rest & (pe2 == v2), lane, big), axis=1, keepdims=True)
    den = v1 + v2
    return (jnp.where(lane == 0, i1.astype(F32), 0.0) + jnp.where(lane == 1, i2.astype(F32), 0.0)
            + jnp.where(lane == 2, g_w * (v1 / den), 0.0) + jnp.where(lane == 3, g_w * (v2 / den), 0.0))


def _merge_kernel(o_ref, cx_ref, g0_ref, g1_ref, x_ref, m_ref, ng_ref, wap_ref, wcp_ref, wout_ref,
                  wrh_ref, wrl_ref, rb_ref, x1_ref, t_ref, route_ref):
    ya = jnp.dot(o_ref[...], wap_ref[...], preferred_element_type=F32)
    yc = jnp.dot(cx_ref[...], wcp_ref[...], preferred_element_type=F32)
    merged = (jax.nn.sigmoid(g0_ref[...].astype(F32)) * ya
              + jax.nn.sigmoid(g1_ref[...].astype(F32)) * yc)
    mix = jnp.dot(merged.astype(BF16), wout_ref[...], preferred_element_type=F32)
    x1 = x_ref[...] + m_ref[0, 2:3, :] * mix
    x1_ref[...] = x1
    ms = jnp.mean(x1 * x1, axis=-1, keepdims=True)
    y = x1 * lax.rsqrt(ms + EPS) * ng_ref[...]
    t = y * (1.0 + m_ref[0, 4:5, :]) + m_ref[0, 3:4, :]
    t_ref[...] = t
    logits = _split_dot(t, wrh_ref[...], wrl_ref[...]) + rb_ref[...]
    route_ref[...] = _route(logits)


def _merge(o2d, cx2d, px2d, x2d, m, ng, wap, wcp, wout, wr_hi, wr_lo, rb, *, rows_per_batch, tm):
    n = x2d.shape[0]
    tiles_per_batch = rows_per_batch // tm
    gblk = GATE_OFF // D_MODEL
    row = lambda i: (i, 0)
    const = lambda i: (0, 0)

    def resident(shape):
        return pl.BlockSpec(shape, const, pipeline_mode=pl.Buffered(1))

    return pl.pallas_call(
        _merge_kernel,
        grid=(n // tm,),
        in_specs=[pl.BlockSpec((tm, ATTN_WIDTH), row),
                  pl.BlockSpec((tm, CONV_CH), row),
                  pl.BlockSpec((tm, D_MODEL), lambda i: (i, gblk)),
                  pl.BlockSpec((tm, D_MODEL), lambda i: (i, gblk + 1)),
                  pl.BlockSpec((tm, D_MODEL), row),
                  pl.BlockSpec((1, N_MOD, D_MODEL), lambda i: (i // tiles_per_batch, 0, 0)),
                  pl.BlockSpec((1, D_MODEL), const),
                  resident((ATTN_WIDTH, D_MODEL)),
                  resident((CONV_CH, D_MODEL)),
                  resident((D_MODEL, D_MODEL)),
                  resident((D_MODEL, LANES)),
                  resident((D_MODEL, LANES)),
                  pl.BlockSpec((1, LANES), const)],
        out_specs=[pl.BlockSpec((tm, D_MODEL), row),
                   pl.BlockSpec((tm, D_MODEL), row),
                   pl.BlockSpec((tm, LANES), row)],
        out_shape=[jax.ShapeDtypeStruct((n, D_MODEL), F32),
                   jax.ShapeDtypeStruct((n, D_MODEL), F32),
                   jax.ShapeDtypeStruct((n, LANES), F32)],
        compiler_params=_params("parallel"),
        name="merge",
    )(o2d, cx2d, px2d, px2d, x2d, m, ng, wap, wcp, wout, wr_hi, wr_lo, rb)


MOE_TM = 256


def _dispatch_plan(route, tm):
    n = route.shape[0]
    n_tiles = (2 * n) // tm + N_EXPERTS
    flat = route[:, 0:2].astype(jnp.int32).reshape(-1)
    onehot = (flat[:, None] == jnp.arange(N_EXPERTS, dtype=jnp.int32)[None, :]).astype(jnp.int32)
    csum = jnp.cumsum(onehot, axis=0)
    rank = jnp.sum((csum - onehot) * onehot, axis=1)
    padded = ((csum[-1] + tm - 1) // tm) * tm
    ends = jnp.cumsum(padded)
    dest = (ends - padded)[flat] + rank
    src = jnp.zeros((n_tiles * tm,), jnp.int32).at[dest].set(jnp.arange(2 * n, dtype=jnp.int32) // 2)
    tile_start = jnp.arange(n_tiles, dtype=jnp.int32) * tm
    tile_expert = jnp.minimum(jnp.sum((tile_start[:, None] >= ends[None, :]).astype(jnp.int32), axis=1),
                              N_EXPERTS - 1)
    n_used = (ends[-1] // tm).reshape(1)
    pos = dest.reshape(n, 2)
    return src.reshape(n_tiles, 1, tm), tile_expert, n_used, pos


def _row_gather(idx_ref, src_hbm, dst_ref, sem, tm):
    for r in range(tm):
        pltpu.make_async_copy(src_hbm.at[pl.ds(idx_ref[0, 0, r], 1)], dst_ref.at[pl.ds(r, 1)], sem).start()


def _row_gather_wait(src_hbm, dst_ref, sem, tm):
    for r in range(tm):
        pltpu.make_async_copy(src_hbm.at[pl.ds(0, 1)], dst_ref.at[pl.ds(r, 1)], sem).wait()


def _expert_kernel(te_ref, nu_ref, src_ref, srcn_ref, t_hbm, w1_ref, w3_ref, w2_ref, y_ref, xbuf, sem, *, tm):
    t = pl.program_id(0)
    nu = nu_ref[0]
    slot = t % 2

    @pl.when(t == 0)
    def _():
        _row_gather(src_ref, t_hbm, xbuf.at[0], sem.at[0], tm)

    @pl.when(t < nu)
    def _():
        _row_gather(srcn_ref, t_hbm, xbuf.at[1 - slot], sem.at[1 - slot], tm)
        _row_gather_wait(t_hbm, xbuf.at[slot], sem.at[slot], tm)
        x = xbuf[slot].astype(BF16)
        h1 = jnp.dot(x, w1_ref[0], preferred_element_type=F32)
        h3 = jnp.dot(x, w3_ref[0], preferred_element_type=F32)
        hid = (h1 * jax.nn.sigmoid(h1)) * h3
        y_ref[...] = jnp.dot(hid.astype(BF16), w2_ref[0], preferred_element_type=F32)

    @pl.when(t == nu - 1)
    def _():
        _row_gather_wait(t_hbm, xbuf.at[1 - slot], sem.at[1 - slot], tm)

    @pl.when(t >= nu)
    def _():
        y_ref[...] = jnp.zeros(y_ref.shape, F32)


def _experts(src, tile_expert, n_used, t2d, w1, w3, w2, *, tm):
    n_tiles = src.shape[0]
    wspec = lambda shape: pl.BlockSpec(shape, lambda t, te, nu: (te[t], 0, 0))
    idx = lambda f: pl.BlockSpec((1, 1, tm), f, memory_space=pltpu.SMEM)
    grid_spec = pltpu.PrefetchScalarGridSpec(
        num_scalar_prefetch=2,
        grid=(n_tiles,),
        in_specs=[idx(lambda t, te, nu: (t, 0, 0)),
                  idx(lambda t, te, nu: (jnp.minimum(t + 1, n_tiles - 1), 0, 0)),
                  pl.BlockSpec(memory_space=pl.ANY),
                  wspec((1, D_MODEL, EXPERT_FF)), wspec((1, D_MODEL, EXPERT_FF)),
                  wspec((1, EXPERT_FF, D_MODEL))],
        out_specs=pl.BlockSpec((tm, D_MODEL), lambda t, te, nu: (t, 0)),
        scratch_shapes=[pltpu.VMEM((2, tm, D_MODEL), F32), pltpu.SemaphoreType.DMA((2,))])
    return pl.pallas_call(
        functools.partial(_expert_kernel, tm=tm),
        grid_spec=grid_spec,
        out_shape=jax.ShapeDtypeStruct((n_tiles * tm, D_MODEL), F32),
        compiler_params=_params("arbitrary"),
        name="experts",
    )(tile_expert, n_used, src, src, t2d, w1, w3, w2)


def _combine_kernel(pa_ref, pb_ref, pan_ref, pbn_ref, y_hbm, x1_ref, route_ref, m_ref, fg_ref, o_ref,
                    ybuf, sem, *, tm, n_tiles):
    i = pl.program_id(0)
    slot = i % 2

    def gather(ia_ref, ib_ref, sl):
        _row_gather(ia_ref, y_hbm, ybuf.at[sl, 0], sem.at[sl], tm)
        _row_gather(ib_ref, y_hbm, ybuf.at[sl, 1], sem.at[sl], tm)

    @pl.when(i == 0)
    def _():
        gather(pa_ref, pb_ref, 0)

    @pl.when(i + 1 < n_tiles)
    def _():
        gather(pan_ref, pbn_ref, 1 - slot)

    _row_gather_wait(y_hbm, ybuf.at[slot, 0], sem.at[slot], tm)
    _row_gather_wait(y_hbm, ybuf.at[slot, 1], sem.at[slot], tm)
    route = route_ref[...]
    y = route[:, 2:3] * ybuf[slot, 0] + route[:, 3:4] * ybuf[slot, 1]
    x2 = x1_ref[...] + m_ref[0, 5:6, :] * y
    ms = jnp.mean(x2 * x2, axis=-1, keepdims=True)
    o_ref[...] = x2 * lax.rsqrt(ms + EPS) * fg_ref[...]


def _combine(pos, y_sorted, x1, route, m, fg, *, rows_per_batch, tm):
    n = x1.shape[0]
    n_tiles = n // tm
    tiles_per_batch = rows_per_batch // tm
    pos_a = pos[:, 0].reshape(n_tiles, 1, tm)
    pos_b = pos[:, 1].reshape(n_tiles, 1, tm)
    cur = lambda i: (i, 0, 0)
    nxt = lambda i: (jnp.minimum(i + 1, n_tiles - 1), 0, 0)
    idx = lambda f: pl.BlockSpec((1, 1, tm), f, memory_space=pltpu.SMEM)
    row = lambda i: (i, 0)
    return pl.pallas_call(
        functools.partial(_combine_kernel, tm=tm, n_tiles=n_tiles),
        grid=(n_tiles,),
        in_specs=[idx(cur), idx(cur), idx(nxt), idx(nxt),
                  pl.BlockSpec(memory_space=pl.ANY),
                  pl.BlockSpec((tm, D_MODEL), row),
                  pl.BlockSpec((tm, LANES), row),
                  pl.BlockSpec((1, N_MOD, D_MODEL), lambda i: (i // tiles_per_batch, 0, 0)),
                  pl.BlockSpec((1, D_MODEL), lambda i: (0, 0))],
        out_specs=pl.BlockSpec((tm, D_MODEL), row),
        out_shape=jax.ShapeDtypeStruct((n, D_MODEL), F32),
        scratch_shapes=[pltpu.VMEM((2, 2, tm, D_MODEL), F32), pltpu.SemaphoreType.DMA((2,))],
        compiler_params=_params("arbitrary"),
        name="combine",
    )(pos_a, pos_b, pos_a, pos_b, y_sorted, x1, route, m, fg)


def kernel(x, c, ctx, c_ctx, w_ada, b_ada, norm_attn_g, norm_ffn_g, w_in, lambda_q1, lambda_k1, lambda_q2, lambda_k2, attn_subln_g, conv_dw_w, conv_dw_b, conv_ln_g, conv_ln_b, w_attn_proj, w_conv_proj, w_out, router_group_w, router_group_b, router_expert_w, router_expert_b, expert_w1, expert_w3, expert_w2, final_norm_g):
    b, s, d = x.shape
    n_ctx = ctx.shape[1]
    depth = w_ada.shape[0]
    assert depth == 1 and d == D_MODEL and b <= 7
    i = 0
    lambda_init = 0.8 - 0.6 * math.exp(-0.3 * i)

    cond = jnp.zeros((8, d), F32).at[:b].set(c).at[b].set(c_ctx)
    mods = _ada(cond, w_ada[i], b_ada[i][None, :])
    m_x = mods[:b].reshape(b, N_MOD, d)
    m_c = mods[b:b + 1].reshape(1, N_MOD, d)

    w_in_bf = w_in[i].astype(BF16)
    tables = _rope_tables(s)
    g_attn = norm_attn_g[i][None, :]
    px = _inproj(x.reshape(b * s, d), m_x, g_attn, w_in_bf, tables,
                 rows_per_batch=s, seq=s, rope_cols=V_OFF, tm=1024, tn=1024)
    pc = _inproj(ctx.reshape(b * n_ctx, d), m_c, g_attn, w_in_bf[:, K_OFF:CONV_OFF], tables,
                 rows_per_batch=b * n_ctx, seq=s, rope_cols=0, tm=n_ctx, tn=1024)
    px3 = px.reshape(b, s, IN_COLS)
    pc3 = pc.reshape(b, n_ctx, CONV_OFF - K_OFF)

    lam_p = jnp.stack([lambda_q1[i], lambda_k1[i], lambda_q2[i], lambda_k2[i]])
    tk = 256
    skv = s + n_ctx
    qt = px3[..., Q_OFF:K_OFF].transpose(0, 2, 1)
    k_all = jnp.concatenate([px3[..., K_OFF:V_OFF], pc3[..., :V_OFF - K_OFF]], axis=1)
    v_all = jnp.concatenate([px3[..., V_OFF:CONV_OFF], pc3[..., V_OFF - K_OFF:]], axis=1)
    vt_all = v_all.reshape(b, skv // tk, tk, N_HEADS, V_DIM).transpose(0, 3, 1, 4, 2)
    vt_all = jnp.concatenate([vt_all, jnp.ones(vt_all.shape[:3] + (SUM_ROWS, tk), BF16)], axis=3)
    o = _attention(qt, k_all, vt_all, lam_p, attn_subln_g[i][None, :], lambda_init=lambda_init, tq=256,
                   unroll=4)
    cx = _conv(px3, conv_dw_w[i], conv_dw_b[i][None, :], conv_ln_g[i][None, :], conv_ln_b[i][None, :],
               ts=512)

    wr = jnp.zeros((d, LANES), F32)
    wr = wr.at[:, :N_EXPERTS].set(router_expert_w[i]).at[:, N_EXPERTS:N_EXPERTS + N_GROUPS].set(router_group_w[i])
    rb = jnp.zeros((1, LANES), F32)
    rb = rb.at[0, :N_EXPERTS].set(router_expert_b[i]).at[0, N_EXPERTS:N_EXPERTS + N_GROUPS].set(router_group_b[i])
    wr_hi = wr.astype(BF16)
    wr_lo = (wr - wr_hi.astype(F32)).astype(BF16)

    x1, t, route = _merge(o.reshape(b * s, ATTN_WIDTH), cx.reshape(b * s, CONV_CH), px, x.reshape(b * s, d),
                          m_x, norm_ffn_g[i][None, :],
                          w_attn_proj[i].astype(BF16), w_conv_proj[i].astype(BF16), w_out[i].astype(BF16),
                          wr_hi, wr_lo, rb, rows_per_batch=s, tm=256)

    src, tile_expert, n_used, pos = _dispatch_plan(route, MOE_TM)
    y_sorted = _experts(src, tile_expert, n_used, t,
                        expert_w1[i].astype(BF16), expert_w3[i].astype(BF16), expert_w2[i].astype(BF16),
                        tm=MOE_TM)
    out = _combine(pos, y_sorted, x1, route, m_x, final_norm_g[None, :], rows_per_batch=s, tm=MOE_TM)
    return out.reshape(b, s, d)
```

```python
import functools
import math

import jax
import jax.numpy as jnp
from jax import lax
from jax.experimental import pallas as pl
from jax.experimental.pallas import tpu as pltpu

D_MODEL = 2048
GRID_W = 64
N_HEADS = 16
HEAD_DIM = 64
V_DIM = 2 * HEAD_DIM
ATTN_WIDTH = N_HEADS * V_DIM
CONV_CH = 1024
CONV_WIDTH = 31
N_GROUPS = 4
EXPERTS_PER_GROUP = 8
N_EXPERTS = N_GROUPS * EXPERTS_PER_GROUP
EXPERT_FF = 512
ROPE_BASE = 10000.0
ROPE_AXIS_DIM = HEAD_DIM // 2
N_MOD = 6
EPS = 1e-6

Q_OFF = 0
K_OFF = Q_OFF + N_HEADS * 2 * HEAD_DIM
V_OFF = K_OFF + N_HEADS * 2 * HEAD_DIM
CONV_OFF = V_OFF + ATTN_WIDTH
GATE_OFF = CONV_OFF + 2 * CONV_CH
IN_COLS = GATE_OFF + 2 * D_MODEL

LANES = 128
CONV_HALO = 16
VMEM_LIMIT = 56 * 1024 * 1024

F32 = jnp.float32
BF16 = jnp.bfloat16


def _params(*sem):
    return pltpu.CompilerParams(dimension_semantics=sem, vmem_limit_bytes=VMEM_LIMIT)


def _ada_kernel(c_ref, w_ref, b_ref, o_ref):
    c = c_ref[...]
    sc = c * jax.nn.sigmoid(c)
    o_ref[...] = jnp.dot(sc, w_ref[...], preferred_element_type=F32,
                         precision=lax.Precision.HIGHEST) + b_ref[...]


def _ada(cond, w, b):
    tn = 1024
    n = w.shape[1]
    return pl.pallas_call(
        _ada_kernel,
        grid=(n // tn,),
        in_specs=[pl.BlockSpec((8, D_MODEL), lambda j: (0, 0)),
                  pl.BlockSpec((D_MODEL, tn), lambda j: (0, j)),
                  pl.BlockSpec((1, tn), lambda j: (0, j))],
        out_specs=pl.BlockSpec((8, tn), lambda j: (0, j)),
        out_shape=jax.ShapeDtypeStruct((8, n), F32),
        compiler_params=_params("parallel"),
        name="ada",
    )(cond, w, b)


def _inproj_kernel(x_ref, m_ref, g_ref, w_ref, cos_ref, sa_ref, sb_ref, o_ref, hx_ref,
                   *, rope_tiles, q_tiles, tn):
    j = pl.program_id(1)

    @pl.when(j == 0)
    def _():
        x = x_ref[...]
        ms = jnp.mean(x * x, axis=-1, keepdims=True)
        y = x * lax.rsqrt(ms + EPS) * g_ref[...]
        shift = m_ref[0, 0:1, :]
        scale = m_ref[0, 1:2, :]
        hx_ref[...] = (y * (1.0 + scale) + shift).astype(BF16)

    acc = jnp.dot(hx_ref[...], w_ref[...], preferred_element_type=F32)

    if rope_tiles == 0:
        o_ref[...] = acc.astype(o_ref.dtype)
        return

    @pl.when(j < rope_tiles)
    def _():
        qs = jnp.where(j < q_tiles, HEAD_DIM ** -0.5 * math.log2(math.e), 1.0).astype(F32)
        cos = cos_ref[...] * qs
        sa = sa_ref[...] * qs
        sb = sb_ref[...] * qs
        for c in range(tn // LANES):
            a = acc[:, c * LANES:(c + 1) * LANES]
            r = (a * cos + pltpu.roll(a, LANES - ROPE_AXIS_DIM // 2, 1) * sa
                 + pltpu.roll(a, ROPE_AXIS_DIM // 2, 1) * sb)
            o_ref[:, c * LANES:(c + 1) * LANES] = r.astype(o_ref.dtype)

    @pl.when(j >= rope_tiles)
    def _():
        o_ref[...] = acc.astype(o_ref.dtype)


def _inproj(x2d, m, g, w, tables, *, rows_per_batch, seq, rope_cols, tm, tn):
    n, c = x2d.shape[0], w.shape[1]
    tiles_per_batch = rows_per_batch // tm
    seq_tiles = max(seq // tm, 1)
    cos, sa, sb = tables
    kern = functools.partial(_inproj_kernel, rope_tiles=rope_cols // tn,
                             q_tiles=(K_OFF - Q_OFF) // tn, tn=tn)
    tab_spec = pl.BlockSpec((tm, LANES), lambda i, j: (i % seq_tiles, 0))
    return pl.pallas_call(
        kern,
        grid=(n // tm, c // tn),
        in_specs=[pl.BlockSpec((tm, D_MODEL), lambda i, j: (i, 0)),
                  pl.BlockSpec((1, N_MOD, D_MODEL), lambda i, j: (i // tiles_per_batch, 0, 0)),
                  pl.BlockSpec((1, D_MODEL), lambda i, j: (0, 0)),
                  pl.BlockSpec((D_MODEL, tn), lambda i, j: (0, j)),
                  tab_spec, tab_spec, tab_spec],
        out_specs=pl.BlockSpec((tm, tn), lambda i, j: (i, j)),
        out_shape=jax.ShapeDtypeStruct((n, c), BF16),
        scratch_shapes=[pltpu.VMEM((tm, D_MODEL), BF16)],
        compiler_params=_params("parallel", "arbitrary"),
        name="inproj",
    )(x2d, m, g, w, cos, sa, sb)


def _rope_tables(seq):
    rows = seq // GRID_W
    row_pos = jnp.repeat(jnp.arange(rows), GRID_W).astype(F32)
    col_pos = jnp.tile(jnp.arange(GRID_W), rows).astype(F32)
    inv = ROPE_BASE ** (-jnp.arange(0, ROPE_AXIS_DIM, 2, dtype=F32) / ROPE_AXIS_DIM)
    ang_r = row_pos[:, None] * inv
    ang_c = col_pos[:, None] * inv
    ang = jnp.concatenate([ang_r, ang_r, ang_c, ang_c] * (LANES // HEAD_DIM), axis=-1)
    cos, sin = jnp.cos(ang), jnp.sin(ang)
    first_half = (jnp.arange(LANES) % ROPE_AXIS_DIM) < ROPE_AXIS_DIM // 2
    sa = jnp.where(first_half, -sin, 0.0)
    sb = jnp.where(first_half, 0.0, sin)
    return cos, sa, sb


SUM_ROWS = 16


def _attn_kernel(lam_ref, g_ref, qt_ref, k_ref, vt_ref, o_ref,
                 qm_ref, s_ref, p_ref, al_ref, m_ref, acc_ref, *, tq, tk, nt, unroll, lambda_init):
    qt = qt_ref[0]
    row = lax.broadcasted_iota(jnp.int32, qt.shape, 0)
    zero = jnp.zeros_like(qt)
    qm_ref[0] = jnp.where(row < HEAD_DIM, qt, zero)
    qm_ref[1] = jnp.where(row >= HEAD_DIM, qt, zero)
    m_ref[...] = jnp.full(m_ref.shape, -jnp.inf, F32)
    acc_ref[...] = jnp.zeros(acc_ref.shape, F32)
    al_ref[...] = jnp.ones(al_ref.shape, F32)
    p_ref[1] = jnp.zeros(p_ref.shape[1:], BF16)

    def scores(j, slot):
        off = pl.multiple_of(j * tk, tk)
        k = k_ref[0, pl.ds(off, tk), :]
        for mp in range(2):
            s_ref[slot, mp] = jnp.dot(k, qm_ref[mp], preferred_element_type=F32).astype(BF16)

    def softmax(slot):
        for mp in range(2):
            s = s_ref[slot, mp]
            m_prev = m_ref[mp]
            m_new = jnp.maximum(m_prev, jnp.max(s, axis=0, keepdims=True).astype(F32))
            al_ref[slot, mp] = jnp.exp2(m_prev - m_new)
            m_ref[mp] = m_new
            p_ref[slot, mp] = jnp.exp2(s - m_new.astype(BF16))

    def pv(j, slot):
        vt = vt_ref[0, 0, j]
        for mp in range(2):
            acc_ref[mp] = (al_ref[slot, mp] * acc_ref[mp]
                           + jnp.dot(vt, p_ref[slot, mp], preferred_element_type=F32))

    scores(0, 0)

    def step(j, par, with_scores=True):
        if with_scores:
            scores(jnp.minimum(j + 1, nt - 1), 1 - par)
        softmax(par)
        pv(jnp.maximum(j - 1, 0), 1 - par)

    def body(t, carry):
        for u in range(unroll):
            step(unroll * t + u, u % 2)
        return carry

    lax.fori_loop(0, nt // unroll, body, 0)
    for j in range(nt - nt % unroll, nt):
        step(j, j % 2, with_scores=j + 1 < nt)
    pv(nt - 1, (nt - 1) % 2)

    lam_p = lam_ref[...]
    lam = (jnp.exp(jnp.sum(lam_p[0:1] * lam_p[1:2], axis=1, keepdims=True))
           - jnp.exp(jnp.sum(lam_p[2:3] * lam_p[3:4], axis=1, keepdims=True)) + lambda_init)
    a0, a1 = acc_ref[0], acc_ref[1]
    ot = a0[:V_DIM] / a0[V_DIM:V_DIM + 1] - lam * (a1[:V_DIM] / a1[V_DIM:V_DIM + 1])
    ms = jnp.mean(ot * ot, axis=0, keepdims=True)
    ot = ot * lax.rsqrt(ms + EPS)
    o = ot.T * (g_ref[...] * (1.0 - lambda_init))
    o_ref[0] = o.astype(o_ref.dtype)


def _attention(qt, k_all, vt_all, lam_p, g, *, lambda_init, tq, unroll):
    b, _, s = qt.shape
    skv = k_all.shape[1]
    nt, vrows, tk = vt_all.shape[2:]
    assert unroll % 2 == 0
    kern = functools.partial(_attn_kernel, tq=tq, tk=tk, nt=nt, unroll=unroll, lambda_init=lambda_init)
    return pl.pallas_call(
        kern,
        grid=(b, N_HEADS, s // tq),
        in_specs=[pl.BlockSpec((4, HEAD_DIM), lambda bb, h, i: (0, 0)),
                  pl.BlockSpec((1, V_DIM), lambda bb, h, i: (0, 0)),
                  pl.BlockSpec((1, LANES, tq), lambda bb, h, i: (bb, h, i)),
                  pl.BlockSpec((1, skv, LANES), lambda bb, h, i: (bb, 0, h)),
                  pl.BlockSpec((1, 1, nt, vrows, tk), lambda bb, h, i: (bb, h, 0, 0, 0))],
        out_specs=pl.BlockSpec((1, tq, LANES), lambda bb, h, i: (bb, i, h)),
        out_shape=jax.ShapeDtypeStruct((b, s, ATTN_WIDTH), BF16),
        scratch_shapes=[pltpu.VMEM((2, LANES, tq), BF16),
                        pltpu.VMEM((2, 2, tk, tq), BF16),
                        pltpu.VMEM((2, 2, tk, tq), BF16),
                        pltpu.VMEM((2, 2, 1, tq), F32),
                        pltpu.VMEM((2, 1, tq), F32),
                        pltpu.VMEM((2, vrows, tq), F32)],
        compiler_params=_params("parallel", "parallel", "arbitrary"),
        name="attn",
    )(lam_p, g, qt, k_all, vt_all)


def _conv_kernel(uc_ref, up_ref, un_ref, w_ref, b_ref, g_ref, be_ref, o_ref, h_ref, *, ts, n_tiles):
    i = pl.program_id(1)

    def glu(u):
        u = u.astype(F32)
        return u[:, :CONV_CH] * jax.nn.sigmoid(u[:, CONV_CH:])

    h_ref[0:CONV_HALO, :] = glu(up_ref[0]) * jnp.where(i > 0, 1.0, 0.0).astype(F32)
    h_ref[CONV_HALO:CONV_HALO + ts, :] = glu(uc_ref[0])
    h_ref[CONV_HALO + ts:, :] = glu(un_ref[0]) * jnp.where(i < n_tiles - 1, 1.0, 0.0).astype(F32)

    w = w_ref[...]
    base = CONV_HALO - CONV_WIDTH // 2
    acc = jnp.zeros((ts, CONV_CH), F32) + b_ref[...]
    for j in range(CONV_WIDTH):
        acc = acc + h_ref[base + j:base + j + ts, :] * w[j:j + 1, :]
    mu = jnp.mean(acc, axis=-1, keepdims=True)
    d = acc - mu
    var = jnp.mean(d * d, axis=-1, keepdims=True)
    y = d * lax.rsqrt(var + EPS) * g_ref[...] + be_ref[...]
    o_ref[0] = (y * jax.nn.sigmoid(y)).astype(o_ref.dtype)


def _conv(px, w, b, g, be, *, ts):
    bsz, s, _ = px.shape
    n_tiles = s // ts
    ublk = CONV_OFF // (2 * CONV_CH)
    hpt = ts // CONV_HALO
    n_halo = s // CONV_HALO
    kern = functools.partial(_conv_kernel, ts=ts, n_tiles=n_tiles)
    vec = pl.BlockSpec((1, CONV_CH), lambda bb, i: (0, 0))
    return pl.pallas_call(
        kern,
        grid=(bsz, n_tiles),
        in_specs=[pl.BlockSpec((1, ts, 2 * CONV_CH), lambda bb, i: (bb, i, ublk)),
                  pl.BlockSpec((1, CONV_HALO, 2 * CONV_CH),
                               lambda bb, i: (bb, jnp.maximum(i * hpt - 1, 0), ublk)),
                  pl.BlockSpec((1, CONV_HALO, 2 * CONV_CH),
                               lambda bb, i: (bb, jnp.minimum((i + 1) * hpt, n_halo - 1), ublk)),
                  pl.BlockSpec((CONV_WIDTH, CONV_CH), lambda bb, i: (0, 0)),
                  vec, vec, vec],
        out_specs=pl.BlockSpec((1, ts, CONV_CH), lambda bb, i: (bb, i, 0)),
        out_shape=jax.ShapeDtypeStruct((bsz, s, CONV_CH), BF16),
        scratch_shapes=[pltpu.VMEM((ts + 2 * CONV_HALO, CONV_CH), F32)],
        compiler_params=_params("parallel", "parallel"),
        name="conv",
    )(px, px, px, w, b, g, be)


def _split_dot(t, w_hi, w_lo):
    t_hi = t.astype(BF16)
    t_lo = (t - t_hi.astype(F32)).astype(BF16)
    return (jnp.dot(t_hi, w_hi, preferred_element_type=F32)
            + jnp.dot(t_hi, w_lo, preferred_element_type=F32)
            + jnp.dot(t_lo, w_hi, preferred_element_type=F32))


def _route(logits):
    lane = lax.broadcasted_iota(jnp.int32, logits.shape, 1)
    big = jnp.int32(1 << 20)
    neg = jnp.float32(-jnp.inf)
    is_g = (lane >= N_EXPERTS) & (lane < N_EXPERTS + N_GROUPS)
    lg = jnp.where(is_g, logits, neg)
    pgu = jnp.exp(lg - jnp.max(lg, axis=1, keepdims=True))
    pg = pgu / jnp.sum(pgu, axis=1, keepdims=True)
    g_w = jnp.max(pg, axis=1, keepdims=True)
    g_idx = jnp.min(jnp.where(is_g & (pg == g_w), lane - N_EXPERTS, big), axis=1, keepdims=True)
    sel = (lane < N_EXPERTS) & ((lane // EXPERTS_PER_GROUP) == g_idx)
    le = jnp.where(sel, logits, neg)
    peu = jnp.exp(le - jnp.max(le, axis=1, keepdims=True))
    pe = peu / jnp.sum(peu, axis=1, keepdims=True)
    v1 = jnp.max(pe, axis=1, keepdims=True)
    i1 = jnp.min(jnp.where(sel & (pe == v1), lane, big), axis=1, keepdims=True)
    rest = sel & (lane != i1)
    pe2 = jnp.where(rest, pe, -1.0)
    v2 = jnp.max(pe2, axis=1, keepdims=True)
    i2 = jnp.min(jnp.where(rest & (pe2 == v2), lane, big), axis=1, keepdims=True)
    den = v1 + v2
    return (jnp.where(lane == 0, i1.astype(F32), 0.0) + jnp.where(lane == 1, i2.astype(F32), 0.0)
            + jnp.where(lane == 2, g_w * (v1 / den), 0.0) + jnp.where(lane == 3, g_w * (v2 / den), 0.0))


def _merge_kernel(o_ref, cx_ref, g0_ref, g1_ref, x_ref, m_ref, ng_ref, wap_ref, wcp_ref, wout_ref,
                  wrh_ref, wrl_ref, rb_ref, x1_ref, t_ref, route_ref):
    ya = jnp.dot(o_ref[...], wap_ref[...], preferred_element_type=F32)
    yc = jnp.dot(cx_ref[...], wcp_ref[...], preferred_element_type=F32)
    merged = (jax.nn.sigmoid(g0_ref[...].astype(F32)) * ya
              + jax.nn.sigmoid(g1_ref[...].astype(F32)) * yc)
    mix = jnp.dot(merged.astype(BF16), wout_ref[...], preferred_element_type=F32)
    x1 = x_ref[...] + m_ref[0, 2:3, :] * mix
    x1_ref[...] = x1
    ms = jnp.mean(x1 * x1, axis=-1, keepdims=True)
    y = x1 * lax.rsqrt(ms + EPS) * ng_ref[...]
    t = y * (1.0 + m_ref[0, 4:5, :]) + m_ref[0, 3:4, :]
    t_ref[...] = t
    logits = _split_dot(t, wrh_ref[...], wrl_ref[...]) + rb_ref[...]
    route_ref[...] = _route(logits)


def _merge(o2d, cx2d, px2d, x2d, m, ng, wap, wcp, wout, wr_hi, wr_lo, rb, *, rows_per_batch, tm):
    n = x2d.shape[0]
    tiles_per_batch = rows_per_batch // tm
    gblk = GATE_OFF // D_MODEL
    row = lambda i: (i, 0)
    const = lambda i: (0, 0)

    def resident(shape):
        return pl.BlockSpec(shape, const, pipeline_mode=pl.Buffered(1))

    return pl.pallas_call(
        _merge_kernel,
        grid=(n // tm,),
        in_specs=[pl.BlockSpec((tm, ATTN_WIDTH), row),
                  pl.BlockSpec((tm, CONV_CH), row),
                  pl.BlockSpec((tm, D_MODEL), lambda i: (i, gblk)),
                  pl.BlockSpec((tm, D_MODEL), lambda i: (i, gblk + 1)),
                  pl.BlockSpec((tm, D_MODEL), row),
                  pl.BlockSpec((1, N_MOD, D_MODEL), lambda i: (i // tiles_per_batch, 0, 0)),
                  pl.BlockSpec((1, D_MODEL), const),
                  resident((ATTN_WIDTH, D_MODEL)),
                  resident((CONV_CH, D_MODEL)),
                  resident((D_MODEL, D_MODEL)),
                  resident((D_MODEL, LANES)),
                  resident((D_MODEL, LANES)),
                  pl.BlockSpec((1, LANES), const)],
        out_specs=[pl.BlockSpec((tm, D_MODEL), row),
                   pl.BlockSpec((tm, D_MODEL), row),
                   pl.BlockSpec((tm, LANES), row)],
        out_shape=[jax.ShapeDtypeStruct((n, D_MODEL), F32),
                   jax.ShapeDtypeStruct((n, D_MODEL), F32),
                   jax.ShapeDtypeStruct((n, LANES), F32)],
        compiler_params=_params("parallel"),
        name="merge",
    )(o2d, cx2d, px2d, px2d, x2d, m, ng, wap, wcp, wout, wr_hi, wr_lo, rb)


MOE_TM = 256


def _dispatch_plan(route, tm):
    n = route.shape[0]
    n_tiles = (2 * n) // tm + N_EXPERTS
    flat = route[:, 0:2].astype(jnp.int32).reshape(-1)
    onehot = (flat[:, None] == jnp.arange(N_EXPERTS, dtype=jnp.int32)[None, :]).astype(jnp.int32)
    csum = jnp.cumsum(onehot, axis=0)
    rank = jnp.sum((csum - onehot) * onehot, axis=1)
    padded = ((csum[-1] + tm - 1) // tm) * tm
    ends = jnp.cumsum(padded)
    dest = (ends - padded)[flat] + rank
    src = jnp.zeros((n_tiles * tm,), jnp.int32).at[dest].set(jnp.arange(2 * n, dtype=jnp.int32) // 2)
    tile_start = jnp.arange(n_tiles, dtype=jnp.int32) * tm
    tile_expert = jnp.minimum(jnp.sum((tile_start[:, None] >= ends[None, :]).astype(jnp.int32), axis=1),
                              N_EXPERTS - 1)
    n_used = (ends[-1] // tm).reshape(1)
    pos = dest.reshape(n, 2)
    return src.reshape(n_tiles, 1, tm), tile_expert, n_used, pos


def _row_gather(idx_ref, src_hbm, dst_ref, sem, tm):
    for r in range(tm):
        pltpu.make_async_copy(src_hbm.at[pl.ds(idx_ref[0, 0, r], 1)], dst_ref.at[pl.ds(r, 1)], sem).start()


def _row_gather_wait(src_hbm, dst_ref, sem, tm):
    for r in range(tm):
        pltpu.make_async_copy(src_hbm.at[pl.ds(0, 1)], dst_ref.at[pl.ds(r, 1)], sem).wait()


def _expert_kernel(te_ref, nu_ref, src_ref, srcn_ref, t_hbm, w1_ref, w3_ref, w2_ref, y_ref, xbuf, sem, *, tm):
    t = pl.program_id(0)
    nu = nu_ref[0]
    slot = t % 2

    @pl.when(t == 0)
    def _():
        _row_gather(src_ref, t_hbm, xbuf.at[0], sem.at[0], tm)

    @pl.when(t < nu)
    def _():
        _row_gather(srcn_ref, t_hbm, xbuf.at[1 - slot], sem.at[1 - slot], tm)
        _row_gather_wait(t_hbm, xbuf.at[slot], sem.at[slot], tm)
        x = xbuf[slot].astype(BF16)
        h1 = jnp.dot(x, w1_ref[0], preferred_element_type=F32)
        h3 = jnp.dot(x, w3_ref[0], preferred_element_type=F32)
        hid = (h1 * jax.nn.sigmoid(h1)) * h3
        y_ref[...] = jnp.dot(hid.astype(BF16), w2_ref[0], preferred_element_type=F32)

    @pl.when(t == nu - 1)
    def _():
        _row_gather_wait(t_hbm, xbuf.at[1 - slot], sem.at[1 - slot], tm)

    @pl.when(t >= nu)
    def _():
        y_ref[...] = jnp.zeros(y_ref.shape, F32)


def _experts(src, tile_expert, n_used, t2d, w1, w3, w2, *, tm):
    n_tiles = src.shape[0]
    wspec = lambda shape: pl.BlockSpec(shape, lambda t, te, nu: (te[t], 0, 0))
    idx = lambda f: pl.BlockSpec((1, 1, tm), f, memory_space=pltpu.SMEM)
    grid_spec = pltpu.PrefetchScalarGridSpec(
        num_scalar_prefetch=2,
        grid=(n_tiles,),
        in_specs=[idx(lambda t, te, nu: (t, 0, 0)),
                  idx(lambda t, te, nu: (jnp.minimum(t + 1, n_tiles - 1), 0, 0)),
                  pl.BlockSpec(memory_space=pl.ANY),
                  wspec((1, D_MODEL, EXPERT_FF)), wspec((1, D_MODEL, EXPERT_FF)),
                  wspec((1, EXPERT_FF, D_MODEL))],
        out_specs=pl.BlockSpec((tm, D_MODEL), lambda t, te, nu: (t, 0)),
        scratch_shapes=[pltpu.VMEM((2, tm, D_MODEL), F32), pltpu.SemaphoreType.DMA((2,))])
    return pl.pallas_call(
        functools.partial(_expert_kernel, tm=tm),
        grid_spec=grid_spec,
        out_shape=jax.ShapeDtypeStruct((n_tiles * tm, D_MODEL), F32),
        compiler_params=_params("arbitrary"),
        name="experts",
    )(tile_expert, n_used, src, src, t2d, w1, w3, w2)


def _combine_kernel(pa_ref, pb_ref, pan_ref, pbn_ref, y_hbm, x1_ref, route_ref, m_ref, fg_ref, o_ref,
                    ybuf, sem, *, tm, n_tiles):
    i = pl.program_id(0)
    slot = i % 2

    def gather(ia_ref, ib_ref, sl):
        _row_gather(ia_ref, y_hbm, ybuf.at[sl, 0], sem.at[sl], tm)
        _row_gather(ib_ref, y_hbm, ybuf.at[sl, 1], sem.at[sl], tm)

    @pl.when(i == 0)
    def _():
        gather(pa_ref, pb_ref, 0)

    @pl.when(i + 1 < n_tiles)
    def _():
        gather(pan_ref, pbn_ref, 1 - slot)

    _row_gather_wait(y_hbm, ybuf.at[slot, 0], sem.at[slot], tm)
    _row_gather_wait(y_hbm, ybuf.at[slot, 1], sem.at[slot], tm)
    route = route_ref[...]
    y = route[:, 2:3] * ybuf[slot, 0] + route[:, 3:4] * ybuf[slot, 1]
    x2 = x1_ref[...] + m_ref[0, 5:6, :] * y
    ms = jnp.mean(x2 * x2, axis=-1, keepdims=True)
    o_ref[...] = x2 * lax.rsqrt(ms + EPS) * fg_ref[...]


def _combine(pos, y_sorted, x1, route, m, fg, *, rows_per_batch, tm):
    n = x1.shape[0]
    n_tiles = n // tm
    tiles_per_batch = rows_per_batch // tm
    pos_a = pos[:, 0].reshape(n_tiles, 1, tm)
    pos_b = pos[:, 1].reshape(n_tiles, 1, tm)
    cur = lambda i: (i, 0, 0)
    nxt = lambda i: (jnp.minimum(i + 1, n_tiles - 1), 0, 0)
    idx = lambda f: pl.BlockSpec((1, 1, tm), f, memory_space=pltpu.SMEM)
    row = lambda i: (i, 0)
    return pl.pallas_call(
        functools.partial(_combine_kernel, tm=tm, n_tiles=n_tiles),
        grid=(n_tiles,),
        in_specs=[idx(cur), idx(cur), idx(nxt), idx(nxt),
                  pl.BlockSpec(memory_space=pl.ANY),
                  pl.BlockSpec((tm, D_MODEL), row),
                  pl.BlockSpec((tm, LANES), row),
                  pl.BlockSpec((1, N_MOD, D_MODEL), lambda i: (i // tiles_per_batch, 0, 0)),
                  pl.BlockSpec((1, D_MODEL), lambda i: (0, 0))],
        out_specs=pl.BlockSpec((tm, D_MODEL), row),
        out_shape=jax.ShapeDtypeStruct((n, D_MODEL), F32),
        scratch_shapes=[pltpu.VMEM((2, 2, tm, D_MODEL), F32), pltpu.SemaphoreType.DMA((2,))],
        compiler_params=_params("arbitrary"),
        name="combine",
    )(pos_a, pos_b, pos_a, pos_b, y_sorted, x1, route, m, fg)


def kernel(x, c, ctx, c_ctx, w_ada, b_ada, norm_attn_g, norm_ffn_g, w_in, lambda_q1, lambda_k1, lambda_q2, lambda_k2, attn_subln_g, conv_dw_w, conv_dw_b, conv_ln_g, conv_ln_b, w_attn_proj, w_conv_proj, w_out, router_group_w, router_group_b, router_expert_w, router_expert_b, expert_w1, expert_w3, expert_w2, final_norm_g):
    b, s, d = x.shape
    n_ctx = ctx.shape[1]
    depth = w_ada.shape[0]
    assert depth == 1 and d == D_MODEL and b <= 7
    i = 0
    lambda_init = 0.8 - 0.6 * math.exp(-0.3 * i)

    cond = jnp.zeros((8, d), F32).at[:b].set(c).at[b].set(c_ctx)
    mods = _ada(cond, w_ada[i], b_ada[i][None, :])
    m_x = mods[:b].reshape(b, N_MOD, d)
    m_c = mods[b:b + 1].reshape(1, N_MOD, d)

    w_in_bf = w_in[i].astype(BF16)
    tables = _rope_tables(s)
    g_attn = norm_attn_g[i][None, :]
    px = _inproj(x.reshape(b * s, d), m_x, g_attn, w_in_bf, tables,
                 rows_per_batch=s, seq=s, rope_cols=V_OFF, tm=1024, tn=1024)
    pc = _inproj(ctx.reshape(b * n_ctx, d), m_c, g_attn, w_in_bf[:, K_OFF:CONV_OFF], tables,
                 rows_per_batch=b * n_ctx, seq=s, rope_cols=0, tm=n_ctx, tn=1024)
    px3 = px.reshape(b, s, IN_COLS)
    pc3 = pc.reshape(b, n_ctx, CONV_OFF - K_OFF)

    lam_p = jnp.stack([lambda_q1[i], lambda_k1[i], lambda_q2[i], lambda_k2[i]])
    tk = 256
    skv = s + n_ctx
    qt = px3[..., Q_OFF:K_OFF].transpose(0, 2, 1)
    k_all = jnp.concatenate([px3[..., K_OFF:V_OFF], pc3[..., :V_OFF - K_OFF]], axis=1)
    v_all = jnp.concatenate([px3[..., V_OFF:CONV_OFF], pc3[..., V_OFF - K_OFF:]], axis=1)
    vt_all = v_all.reshape(b, skv // tk, tk, N_HEADS, V_DIM).transpose(0, 3, 1, 4, 2)
    vt_all = jnp.concatenate([vt_all, jnp.ones(vt_all.shape[:3] + (SUM_ROWS, tk), BF16)], axis=3)
    o = _attention(qt, k_all, vt_all, lam_p, attn_subln_g[i][None, :], lambda_init=lambda_init, tq=256,
                   unroll=32)
    cx = _conv(px3, conv_dw_w[i], conv_dw_b[i][None, :], conv_ln_g[i][None, :], conv_ln_b[i][None, :],
               ts=512)

    wr = jnp.zeros((d, LANES), F32)
    wr = wr.at[:, :N_EXPERTS].set(router_expert_w[i]).at[:, N_EXPERTS:N_EXPERTS + N_GROUPS].set(router_group_w[i])
    rb = jnp.zeros((1, LANES), F32)
    rb = rb.at[0, :N_EXPERTS].set(router_expert_b[i]).at[0, N_EXPERTS:N_EXPERTS + N_GROUPS].set(router_group_b[i])
    wr_hi = wr.astype(BF16)
    wr_lo = (wr - wr_hi.astype(F32)).astype(BF16)

    x1, t, route = _merge(o.reshape(b * s, ATTN_WIDTH), cx.reshape(b * s, CONV_CH), px, x.reshape(b * s, d),
                          m_x, norm_ffn_g[i][None, :],
                          w_attn_proj[i].astype(BF16), w_conv_proj[i].astype(BF16), w_out[i].astype(BF16),
                          wr_hi, wr_lo, rb, rows_per_batch=s, tm=256)

    src, tile_expert, n_used, pos = _dispatch_plan(route, MOE_TM)
    y_sorted = _experts(src, tile_expert, n_used, t,
                        expert_w1[i].astype(BF16), expert_w3[i].astype(BF16), expert_w2[i].astype(BF16),
                        tm=MOE_TM)
    out = _combine(pos, y_sorted, x1, route, m_x, final_norm_g[None, :], rows_per_batch=s, tm=MOE_TM)
    return out.reshape(b, s, d)
```

```python
import functools
import math

import jax
import jax.numpy as jnp
from jax import lax
from jax.experimental import pallas as pl
from jax.experimental.pallas import tpu as pltpu

D_MODEL = 2048
GRID_W = 64
N_HEADS = 16
HEAD_DIM = 64
V_DIM = 2 * HEAD_DIM
ATTN_WIDTH = N_HEADS * V_DIM
CONV_CH = 1024
CONV_WIDTH = 31
N_GROUPS = 4
EXPERTS_PER_GROUP = 8
N_EXPERTS = N_GROUPS * EXPERTS_PER_GROUP
EXPERT_FF = 512
ROPE_BASE = 10000.0
ROPE_AXIS_DIM = HEAD_DIM // 2
N_MOD = 6
EPS = 1e-6

Q_OFF = 0
K_OFF = Q_OFF + N_HEADS * 2 * HEAD_DIM
V_OFF = K_OFF + N_HEADS * 2 * HEAD_DIM
CONV_OFF = V_OFF + ATTN_WIDTH
GATE_OFF = CONV_OFF + 2 * CONV_CH
IN_COLS = GATE_OFF + 2 * D_MODEL

LANES = 128
SUBLANES = 8
CONV_HALO = 16
CONV_ROWS = 16
VMEM_LIMIT = 56 * 1024 * 1024

F32 = jnp.float32
BF16 = jnp.bfloat16


def _params(*sem):
    return pltpu.CompilerParams(dimension_semantics=sem, vmem_limit_bytes=VMEM_LIMIT)


def _ada_kernel(c_ref, w_ref, b_ref, o_ref):
    c = c_ref[...]
    sc = c * jax.nn.sigmoid(c)
    o_ref[...] = jnp.dot(sc, w_ref[...], preferred_element_type=F32,
                         precision=lax.Precision.HIGHEST) + b_ref[...]


def _ada(cond, w, b):
    tn = 1024
    n = w.shape[1]
    return pl.pallas_call(
        _ada_kernel,
        grid=(n // tn,),
        in_specs=[pl.BlockSpec((8, D_MODEL), lambda j: (0, 0)),
                  pl.BlockSpec((D_MODEL, tn), lambda j: (0, j)),
                  pl.BlockSpec((1, tn), lambda j: (0, j))],
        out_specs=pl.BlockSpec((8, tn), lambda j: (0, j)),
        out_shape=jax.ShapeDtypeStruct((8, n), F32),
        compiler_params=_params("parallel"),
        name="ada",
    )(cond, w, b)


def _inproj_kernel(x_ref, m_ref, g_ref, w_ref, cos_ref, sa_ref, sb_ref, o_ref, hx_ref,
                   *, rope_tiles, q_tiles, tn):
    j = pl.program_id(1)

    @pl.when(j == 0)
    def _():
        x = x_ref[...]
        ms = jnp.mean(x * x, axis=-1, keepdims=True)
        y = x * lax.rsqrt(ms + EPS) * g_ref[...]
        shift = m_ref[0, 0:1, :]
        scale = m_ref[0, 1:2, :]
        hx_ref[...] = (y * (1.0 + scale) + shift).astype(BF16)

    def project():
        return jnp.dot(hx_ref[...], w_ref[...], preferred_element_type=F32)

    if rope_tiles == 0:
        o_ref[...] = project().astype(o_ref.dtype)
        return

    @pl.when(j < rope_tiles)
    def _():
        acc = project()
        qs = jnp.where(j < q_tiles, HEAD_DIM ** -0.5 * math.log2(math.e), 1.0).astype(F32)
        cos = cos_ref[...] * qs
        sa = sa_ref[...] * qs
        sb = sb_ref[...] * qs
        for c in range(tn // LANES):
            a = acc[:, c * LANES:(c + 1) * LANES]
            r = (a * cos + pltpu.roll(a, LANES - ROPE_AXIS_DIM // 2, 1) * sa
                 + pltpu.roll(a, ROPE_AXIS_DIM // 2, 1) * sb)
            o_ref[:, c * LANES:(c + 1) * LANES] = r.astype(o_ref.dtype)

    @pl.when(j >= rope_tiles)
    def _():
        o_ref[...] = project().astype(o_ref.dtype)


def _inproj(x2d, m, g, w, tables, *, rows_per_batch, seq, rope_cols, tm, tn):
    n, c = x2d.shape[0], w.shape[1]
    tiles_per_batch = rows_per_batch // tm
    seq_tiles = max(seq // tm, 1)
    cos, sa, sb = tables
    kern = functools.partial(_inproj_kernel, rope_tiles=rope_cols // tn,
                             q_tiles=(K_OFF - Q_OFF) // tn, tn=tn)
    tab_spec = pl.BlockSpec((tm, LANES), lambda i, j: (i % seq_tiles, 0))
    return pl.pallas_call(
        kern,
        grid=(n // tm, c // tn),
        in_specs=[pl.BlockSpec((tm, D_MODEL), lambda i, j: (i, 0)),
                  pl.BlockSpec((1, N_MOD, D_MODEL), lambda i, j: (i // tiles_per_batch, 0, 0)),
                  pl.BlockSpec((1, D_MODEL), lambda i, j: (0, 0)),
                  pl.BlockSpec((D_MODEL, tn), lambda i, j: (0, j)),
                  tab_spec, tab_spec, tab_spec],
        out_specs=pl.BlockSpec((tm, tn), lambda i, j: (i, j)),
        out_shape=jax.ShapeDtypeStruct((n, c), BF16),
        scratch_shapes=[pltpu.VMEM((tm, D_MODEL), BF16)],
        compiler_params=_params("parallel", "arbitrary"),
        name="inproj",
    )(x2d, m, g, w, cos, sa, sb)


def _rope_tables(seq):
    rows = seq // GRID_W
    row_pos = jnp.repeat(jnp.arange(rows), GRID_W).astype(F32)
    col_pos = jnp.tile(jnp.arange(GRID_W), rows).astype(F32)
    inv = ROPE_BASE ** (-jnp.arange(0, ROPE_AXIS_DIM, 2, dtype=F32) / ROPE_AXIS_DIM)
    ang_r = row_pos[:, None] * inv
    ang_c = col_pos[:, None] * inv
    ang = jnp.concatenate([ang_r, ang_r, ang_c, ang_c] * (LANES // HEAD_DIM), axis=-1)
    cos, sin = jnp.cos(ang), jnp.sin(ang)
    first_half = (jnp.arange(LANES) % ROPE_AXIS_DIM) < ROPE_AXIS_DIM // 2
    sa = jnp.where(first_half, -sin, 0.0)
    sb = jnp.where(first_half, 0.0, sin)
    return cos, sa, sb


SUM_ROWS = 16


def _attn_kernel(lam_ref, g_ref, qt_ref, k_ref, vt_ref, o_ref,
                 qm_ref, s_ref, p_ref, al_ref, m_ref, acc_ref, *, tq, tk, nt, unroll, lambda_init):
    qt = qt_ref[0]
    row = lax.broadcasted_iota(jnp.int32, qt.shape, 0)
    zero = jnp.zeros_like(qt)
    qm_ref[0] = jnp.where(row < HEAD_DIM, qt, zero)
    qm_ref[1] = jnp.where(row >= HEAD_DIM, qt, zero)
    m_ref[...] = jnp.full(m_ref.shape, -jnp.inf, F32)
    acc_ref[...] = jnp.zeros(acc_ref.shape, F32)
    al_ref[...] = jnp.ones(al_ref.shape, F32)
    p_ref[1] = jnp.zeros(p_ref.shape[1:], BF16)

    def scores(j, slot):
        off = pl.multiple_of(j * tk, tk)
        k = k_ref[0, pl.ds(off, tk), :]
        for mp in range(2):
            s_ref[slot, mp] = jnp.dot(k, qm_ref[mp], preferred_element_type=F32).astype(BF16)

    def softmax(slot):
        for mp in range(2):
            s = s_ref[slot, mp]
            m_prev = m_ref[mp]
            m_new = jnp.maximum(m_prev, jnp.max(s, axis=0, keepdims=True).astype(F32))
            al_ref[slot, mp] = jnp.exp2(m_prev - m_new)
            m_ref[mp] = m_new
            p_ref[slot, mp] = jnp.exp2(s - m_new.astype(BF16))

    def pv(j, slot):
        vt = vt_ref[0, 0, j]
        for mp in range(2):
            acc_ref[mp] = (al_ref[slot, mp] * acc_ref[mp]
                           + jnp.dot(vt, p_ref[slot, mp], preferred_element_type=F32))

    scores(0, 0)

    def step(j, par, with_scores=True):
        if with_scores:
            scores(jnp.minimum(j + 1, nt - 1), 1 - par)
        softmax(par)
        pv(jnp.maximum(j - 1, 0), 1 - par)

    def body(t, carry):
        for u in range(unroll):
            step(unroll * t + u, u % 2)
        return carry

    lax.fori_loop(0, nt // unroll, body, 0)
    for j in range(nt - nt % unroll, nt):
        step(j, j % 2, with_scores=j + 1 < nt)
    pv(nt - 1, (nt - 1) % 2)

    lam_p = lam_ref[...]
    lam = (jnp.exp(jnp.sum(lam_p[0:1] * lam_p[1:2], axis=1, keepdims=True))
           - jnp.exp(jnp.sum(lam_p[2:3] * lam_p[3:4], axis=1, keepdims=True)) + lambda_init)
    a0, a1 = acc_ref[0], acc_ref[1]
    ot = a0[:V_DIM] / a0[V_DIM:V_DIM + 1] - lam * (a1[:V_DIM] / a1[V_DIM:V_DIM + 1])
    ms = jnp.mean(ot * ot, axis=0, keepdims=True)
    ot = ot * lax.rsqrt(ms + EPS)
    o = ot.T * (g_ref[...] * (1.0 - lambda_init))
    o_ref[0] = o.astype(o_ref.dtype)


def _attention(qt, k_all, vt_all, lam_p, g, *, lambda_init, tq, unroll):
    b, _, s = qt.shape
    skv = k_all.shape[1]
    nt, vrows, tk = vt_all.shape[2:]
    assert unroll % 2 == 0
    kern = functools.partial(_attn_kernel, tq=tq, tk=tk, nt=nt, unroll=unroll, lambda_init=lambda_init)
    return pl.pallas_call(
        kern,
        grid=(b, N_HEADS, s // tq),
        in_specs=[pl.BlockSpec((4, HEAD_DIM), lambda bb, h, i: (0, 0)),
                  pl.BlockSpec((1, V_DIM), lambda bb, h, i: (0, 0)),
                  pl.BlockSpec((1, LANES, tq), lambda bb, h, i: (bb, h, i)),
                  pl.BlockSpec((1, skv, LANES), lambda bb, h, i: (bb, 0, h)),
                  pl.BlockSpec((1, 1, nt, vrows, tk), lambda bb, h, i: (bb, h, 0, 0, 0))],
        out_specs=pl.BlockSpec((1, tq, LANES), lambda bb, h, i: (bb, i, h)),
        out_shape=jax.ShapeDtypeStruct((b, s, ATTN_WIDTH), BF16),
        scratch_shapes=[pltpu.VMEM((2, LANES, tq), BF16),
                        pltpu.VMEM((2, 2, tk, tq), BF16),
                        pltpu.VMEM((2, 2, tk, tq), BF16),
                        pltpu.VMEM((2, 2, 1, tq), F32),
                        pltpu.VMEM((2, 1, tq), F32),
                        pltpu.VMEM((2, vrows, tq), F32)],
        compiler_params=_params("parallel", "parallel", "arbitrary"),
        name="attn",
    )(lam_p, g, qt, k_all, vt_all)


def _conv_kernel(uc_ref, up_ref, un_ref, w_ref, b_ref, g_ref, be_ref, o_ref, h_ref, acc_ref, wb_ref,
                 *, ts, n_tiles):
    i = pl.program_id(1)

    def glu(u):
        u = u.astype(F32)
        return u[:, :CONV_CH] * jax.nn.sigmoid(u[:, CONV_CH:])

    h_ref[0, 0:CONV_HALO, :] = glu(up_ref[0]) * jnp.where(i > 0, 1.0, 0.0).astype(F32)
    h_ref[0, CONV_HALO:CONV_HALO + ts, :] = glu(uc_ref[0])
    h_ref[0, CONV_HALO + ts:, :] = glu(un_ref[0]) * jnp.where(i < n_tiles - 1, 1.0, 0.0).astype(F32)
    rows = ts + 2 * CONV_HALO - SUBLANES
    for r in range(1, SUBLANES):
        h_ref[r, 0:rows, :] = h_ref[0, r:r + rows, :]

    for j in range(CONV_WIDTH):
        wb_ref[j] = jnp.broadcast_to(w_ref[j:j + 1, :], (CONV_ROWS, CONV_CH))
    base = CONV_HALO - CONV_WIDTH // 2

    def chunk(c, carry):
        r0 = pl.multiple_of(c * CONV_ROWS, CONV_ROWS)
        acc = jnp.zeros((CONV_ROWS, CONV_CH), F32) + b_ref[...]
        for j in range(CONV_WIDTH):
            q, r = divmod(base + j, SUBLANES)
            acc = acc + h_ref[r, pl.ds(r0 + SUBLANES * q, CONV_ROWS), :] * wb_ref[j]
        acc_ref[pl.ds(r0, CONV_ROWS), :] = acc
        return carry

    lax.fori_loop(0, ts // CONV_ROWS, chunk, 0)
    acc = acc_ref[...]
    mu = jnp.mean(acc, axis=-1, keepdims=True)
    d = acc - mu
    var = jnp.mean(d * d, axis=-1, keepdims=True)
    y = d * lax.rsqrt(var + EPS) * g_ref[...] + be_ref[...]
    o_ref[0] = (y * jax.nn.sigmoid(y)).astype(o_ref.dtype)


def _conv(px, w, b, g, be, *, ts):
    bsz, s, _ = px.shape
    n_tiles = s // ts
    ublk = CONV_OFF // (2 * CONV_CH)
    hpt = ts // CONV_HALO
    n_halo = s // CONV_HALO
    kern = functools.partial(_conv_kernel, ts=ts, n_tiles=n_tiles)
    vec = pl.BlockSpec((1, CONV_CH), lambda bb, i: (0, 0))
    return pl.pallas_call(
        kern,
        grid=(bsz, n_tiles),
        in_specs=[pl.BlockSpec((1, ts, 2 * CONV_CH), lambda bb, i: (bb, i, ublk)),
                  pl.BlockSpec((1, CONV_HALO, 2 * CONV_CH),
                               lambda bb, i: (bb, jnp.maximum(i * hpt - 1, 0), ublk)),
                  pl.BlockSpec((1, CONV_HALO, 2 * CONV_CH),
                               lambda bb, i: (bb, jnp.minimum((i + 1) * hpt, n_halo - 1), ublk)),
                  pl.BlockSpec((CONV_WIDTH, CONV_CH), lambda bb, i: (0, 0)),
                  vec, vec, vec],
        out_specs=pl.BlockSpec((1, ts, CONV_CH), lambda bb, i: (bb, i, 0)),
        out_shape=jax.ShapeDtypeStruct((bsz, s, CONV_CH), BF16),
        scratch_shapes=[pltpu.VMEM((SUBLANES, ts + 2 * CONV_HALO, CONV_CH), F32),
                        pltpu.VMEM((ts, CONV_CH), F32),
                        pltpu.VMEM((CONV_WIDTH, CONV_ROWS, CONV_CH), F32)],
        compiler_params=_params("parallel", "parallel"),
        name="conv",
    )(px, px, px, w, b, g, be)


def _split_dot(t, w_hi, w_lo):
    t_hi = t.astype(BF16)
    t_lo = (t - t_hi.astype(F32)).astype(BF16)
    return (jnp.dot(t_hi, w_hi, preferred_element_type=F32)
            + jnp.dot(t_hi, w_lo, preferred_element_type=F32)
            + jnp.dot(t_lo, w_hi, preferred_element_type=F32))


def _route(logits):
    lane = lax.broadcasted_iota(jnp.int32, logits.shape, 1)
    big = jnp.int32(1 << 20)
    neg = jnp.float32(-jnp.inf)
    is_g = (lane >= N_EXPERTS) & (lane < N_EXPERTS + N_GROUPS)
    lg = jnp.where(is_g, logits, neg)
    pgu = jnp.exp(lg - jnp.max(lg, axis=1, keepdims=True))
    pg = pgu / jnp.sum(pgu, axis=1, keepdims=True)
    g_w = jnp.max(pg, axis=1, keepdims=True)
    g_idx = jnp.min(jnp.where(is_g & (pg == g_w), lane - N_EXPERTS, big), axis=1, keepdims=True)
    sel = (lane < N_EXPERTS) & ((lane // EXPERTS_PER_GROUP) == g_idx)
    le = jnp.where(sel, logits, neg)
    peu = jnp.exp(le - jnp.max(le, axis=1, keepdims=True))
    pe = peu / jnp.sum(peu, axis=1, keepdims=True)
    v1 = jnp.max(pe, axis=1, keepdims=True)
    i1 = jnp.min(jnp.where(sel & (pe == v1), lane, big), axis=1, keepdims=True)
    rest = sel & (lane != i1)
    pe2 = jnp.where(rest, pe, -1.0)
    v2 = jnp.max(pe2, axis=1, keepdims=True)
    i2 = jnp.min(jnp.where(rest & (pe2 == v2), lane, big), axis=1, keepdims=True)
    den = v1 + v2
    return (jnp.where(lane == 0, i1.astype(F32), 0.0) + jnp.where(lane == 1, i2.astype(F32), 0.0)
            + jnp.where(lane == 2, g_w * (v1 / den), 0.0) + jnp.where(lane == 3, g_w * (v2 / den), 0.0))


def _merge_kernel(o_ref, cx_ref, g0_ref, g1_ref, x_ref, m_ref, ng_ref, wap_ref, wcp_ref, wout_ref,
                  wrh_ref, wrl_ref, rb_ref, x1_ref, t_ref, route_ref):
    ya = jnp.dot(o_ref[...], wap_ref[...], preferred_element_type=F32)
    yc = jnp.dot(cx_ref[...], wcp_ref[...], preferred_element_type=F32)
    merged = (jax.nn.sigmoid(g0_ref[...].astype(F32)) * ya
              + jax.nn.sigmoid(g1_ref[...].astype(F32)) * yc)
    mix = jnp.dot(merged.astype(BF16), wout_ref[...], preferred_element_type=F32)
    x1 = x_ref[...] + m_ref[0, 2:3, :] * mix
    x1_ref[...] = x1
    ms = jnp.mean(x1 * x1, axis=-1, keepdims=True)
    y = x1 * lax.rsqrt(ms + EPS) * ng_ref[...]
    t = y * (1.0 + m_ref[0, 4:5, :]) + m_ref[0, 3:4, :]
    t_ref[...] = t
    logits = _split_dot(t, wrh_ref[...], wrl_ref[...]) + rb_ref[...]
    route_ref[...] = _route(logits)


def _merge(o2d, cx2d, px2d, x2d, m, ng, wap, wcp, wout, wr_hi, wr_lo, rb, *, rows_per_batch, tm):
    n = x2d.shape[0]
    tiles_per_batch = rows_per_batch // tm
    gblk = GATE_OFF // D_MODEL
    row = lambda i: (i, 0)
    const = lambda i: (0, 0)

    def resident(shape):
        return pl.BlockSpec(shape, const, pipeline_mode=pl.Buffered(1))

    return pl.pallas_call(
        _merge_kernel,
        grid=(n // tm,),
        in_specs=[pl.BlockSpec((tm, ATTN_WIDTH), row),
                  pl.BlockSpec((tm, CONV_CH), row),
                  pl.BlockSpec((tm, D_MODEL), lambda i: (i, gblk)),
                  pl.BlockSpec((tm, D_MODEL), lambda i: (i, gblk + 1)),
                  pl.BlockSpec((tm, D_MODEL), row),
                  pl.BlockSpec((1, N_MOD, D_MODEL), lambda i: (i // tiles_per_batch, 0, 0)),
                  pl.BlockSpec((1, D_MODEL), const),
                  resident((ATTN_WIDTH, D_MODEL)),
                  resident((CONV_CH, D_MODEL)),
                  resident((D_MODEL, D_MODEL)),
                  resident((D_MODEL, LANES)),
                  resident((D_MODEL, LANES)),
                  pl.BlockSpec((1, LANES), const)],
        out_specs=[pl.BlockSpec((tm, D_MODEL), row),
                   pl.BlockSpec((tm, D_MODEL), row),
                   pl.BlockSpec((tm, LANES), row)],
        out_shape=[jax.ShapeDtypeStruct((n, D_MODEL), F32),
                   jax.ShapeDtypeStruct((n, D_MODEL), F32),
                   jax.ShapeDtypeStruct((n, LANES), F32)],
        compiler_params=_params("parallel"),
        name="merge",
    )(o2d, cx2d, px2d, px2d, x2d, m, ng, wap, wcp, wout, wr_hi, wr_lo, rb)


MOE_TM = 256


def _dispatch_plan(route, tm):
    n = route.shape[0]
    n_tiles = (2 * n) // tm + N_EXPERTS
    flat = route[:, 0:2].astype(jnp.int32).reshape(-1)
    onehot = (flat[:, None] == jnp.arange(N_EXPERTS, dtype=jnp.int32)[None, :]).astype(jnp.int32)
    blk = LANES
    tri = (jnp.arange(blk)[:, None] >= jnp.arange(blk)[None, :]).astype(BF16)
    within = jnp.einsum('ij,bjk->bik', tri, onehot.reshape(-1, blk, N_EXPERTS).astype(BF16),
                        preferred_element_type=F32).astype(jnp.int32)
    totals = within[:, -1, :]
    csum = (within + (jnp.cumsum(totals, axis=0) - totals)[:, None, :]).reshape(2 * n, N_EXPERTS)
    rank = jnp.sum((csum - onehot) * onehot, axis=1)
    padded = ((csum[-1] + tm - 1) // tm) * tm
    ends = jnp.cumsum(padded)
    dest = (ends - padded)[flat] + rank
    src = jnp.zeros((n_tiles * tm,), jnp.int32).at[dest].set(jnp.arange(2 * n, dtype=jnp.int32) // 2)
    tile_start = jnp.arange(n_tiles, dtype=jnp.int32) * tm
    tile_expert = jnp.minimum(jnp.sum((tile_start[:, None] >= ends[None, :]).astype(jnp.int32), axis=1),
                              N_EXPERTS - 1)
    n_used = (ends[-1] // tm).reshape(1)
    pos = dest.reshape(n, 2)
    return src.reshape(n_tiles, 1, tm), tile_expert, n_used, pos


def _row_gather(idx_ref, src_hbm, dst_ref, sem, tm):
    for r in range(tm):
        pltpu.make_async_copy(src_hbm.at[pl.ds(idx_ref[0, 0, r], 1)], dst_ref.at[pl.ds(r, 1)], sem).start()


def _row_gather_wait(src_hbm, dst_ref, sem, tm):
    for r in range(tm):
        pltpu.make_async_copy(src_hbm.at[pl.ds(0, 1)], dst_ref.at[pl.ds(r, 1)], sem).wait()


def _expert_kernel(te_ref, nu_ref, src_ref, srcn_ref, t_hbm, w1_ref, w3_ref, w2_ref, y_ref, xbuf, sem, *, tm):
    t = pl.program_id(0)
    nu = nu_ref[0]
    slot = t % 2

    @pl.when(t == 0)
    def _():
        _row_gather(src_ref, t_hbm, xbuf.at[0], sem.at[0], tm)

    @pl.when(t < nu)
    def _():
        _row_gather(srcn_ref, t_hbm, xbuf.at[1 - slot], sem.at[1 - slot], tm)
        _row_gather_wait(t_hbm, xbuf.at[slot], sem.at[slot], tm)
        x = xbuf[slot].astype(BF16)
        h1 = jnp.dot(x, w1_ref[0], preferred_element_type=F32)
        h3 = jnp.dot(x, w3_ref[0], preferred_element_type=F32)
        hid = (h1 * jax.nn.sigmoid(h1)) * h3
        y_ref[...] = jnp.dot(hid.astype(BF16), w2_ref[0], preferred_element_type=F32)

    @pl.when(t == nu - 1)
    def _():
        _row_gather_wait(t_hbm, xbuf.at[1 - slot], sem.at[1 - slot], tm)

    @pl.when(t >= nu)
    def _():
        y_ref[...] = jnp.zeros(y_ref.shape, F32)


def _experts(src, tile_expert, n_used, t2d, w1, w3, w2, *, tm):
    n_tiles = src.shape[0]
    wspec = lambda shape: pl.BlockSpec(shape, lambda t, te, nu: (te[t], 0, 0))
    idx = lambda f: pl.BlockSpec((1, 1, tm), f, memory_space=pltpu.SMEM)
    grid_spec = pltpu.PrefetchScalarGridSpec(
        num_scalar_prefetch=2,
        grid=(n_tiles,),
        in_specs=[idx(lambda t, te, nu: (t, 0, 0)),
                  idx(lambda t, te, nu: (jnp.minimum(t + 1, n_tiles - 1), 0, 0)),
                  pl.BlockSpec(memory_space=pl.ANY),
                  wspec((1, D_MODEL, EXPERT_FF)), wspec((1, D_MODEL, EXPERT_FF)),
                  wspec((1, EXPERT_FF, D_MODEL))],
        out_specs=pl.BlockSpec((tm, D_MODEL), lambda t, te, nu: (t, 0)),
        scratch_shapes=[pltpu.VMEM((2, tm, D_MODEL), F32), pltpu.SemaphoreType.DMA((2,))])
    return pl.pallas_call(
        functools.partial(_expert_kernel, tm=tm),
        grid_spec=grid_spec,
        out_shape=jax.ShapeDtypeStruct((n_tiles * tm, D_MODEL), F32),
        compiler_params=_params("arbitrary"),
        name="experts",
    )(tile_expert, n_used, src, src, t2d, w1, w3, w2)


def _combine_kernel(pa_ref, pb_ref, pan_ref, pbn_ref, y_hbm, x1_ref, route_ref, m_ref, fg_ref, o_ref,
                    ybuf, sem, *, tm, n_tiles):
    i = pl.program_id(0)
    slot = i % 2

    def gather(ia_ref, ib_ref, sl):
        _row_gather(ia_ref, y_hbm, ybuf.at[sl, 0], sem.at[sl], tm)
        _row_gather(ib_ref, y_hbm, ybuf.at[sl, 1], sem.at[sl], tm)

    @pl.when(i == 0)
    def _():
        gather(pa_ref, pb_ref, 0)

    @pl.when(i + 1 < n_tiles)
    def _():
        gather(pan_ref, pbn_ref, 1 - slot)

    _row_gather_wait(y_hbm, ybuf.at[slot, 0], sem.at[slot], tm)
    _row_gather_wait(y_hbm, ybuf.at[slot, 1], sem.at[slot], tm)
    route = route_ref[...]
    y = route[:, 2:3] * ybuf[slot, 0] + route[:, 3:4] * ybuf[slot, 1]
    x2 = x1_ref[...] + m_ref[0, 5:6, :] * y
    ms = jnp.mean(x2 * x2, axis=-1, keepdims=True)
    o_ref[...] = x2 * lax.rsqrt(ms + EPS) * fg_ref[...]


def _combine(pos, y_sorted, x1, route, m, fg, *, rows_per_batch, tm):
    n = x1.shape[0]
    n_tiles = n // tm
    tiles_per_batch = rows_per_batch // tm
    pos_a = pos[:, 0].reshape(n_tiles, 1, tm)
    pos_b = pos[:, 1].reshape(n_tiles, 1, tm)
    cur = lambda i: (i, 0, 0)
    nxt = lambda i: (jnp.minimum(i + 1, n_tiles - 1), 0, 0)
    idx = lambda f: pl.BlockSpec((1, 1, tm), f, memory_space=pltpu.SMEM)
    row = lambda i: (i, 0)
    return pl.pallas_call(
        functools.partial(_combine_kernel, tm=tm, n_tiles=n_tiles),
        grid=(n_tiles,),
        in_specs=[idx(cur), idx(cur), idx(nxt), idx(nxt),
                  pl.BlockSpec(memory_space=pl.ANY),
                  pl.BlockSpec((tm, D_MODEL), row),
                  pl.BlockSpec((tm, LANES), row),
                  pl.BlockSpec((1, N_MOD, D_MODEL), lambda i: (i // tiles_per_batch, 0, 0)),
                  pl.BlockSpec((1, D_MODEL), lambda i: (0, 0))],
        out_specs=pl.BlockSpec((tm, D_MODEL), row),
        out_shape=jax.ShapeDtypeStruct((n, D_MODEL), F32),
        scratch_shapes=[pltpu.VMEM((2, 2, tm, D_MODEL), F32), pltpu.SemaphoreType.DMA((2,))],
        compiler_params=_params("arbitrary"),
        name="combine",
    )(pos_a, pos_b, pos_a, pos_b, y_sorted, x1, route, m, fg)


def kernel(x, c, ctx, c_ctx, w_ada, b_ada, norm_attn_g, norm_ffn_g, w_in, lambda_q1, lambda_k1, lambda_q2, lambda_k2, attn_subln_g, conv_dw_w, conv_dw_b, conv_ln_g, conv_ln_b, w_attn_proj, w_conv_proj, w_out, router_group_w, router_group_b, router_expert_w, router_expert_b, expert_w1, expert_w3, expert_w2, final_norm_g):
    b, s, d = x.shape
    n_ctx = ctx.shape[1]
    depth = w_ada.shape[0]
    assert depth == 1 and d == D_MODEL and b <= 7
    i = 0
    lambda_init = 0.8 - 0.6 * math.exp(-0.3 * i)

    cond = jnp.zeros((8, d), F32).at[:b].set(c).at[b].set(c_ctx)
    mods = _ada(cond, w_ada[i], b_ada[i][None, :])
    m_x = mods[:b].reshape(b, N_MOD, d)
    m_c = mods[b:b + 1].reshape(1, N_MOD, d)

    w_in_bf = w_in[i].astype(BF16)
    tables = _rope_tables(s)
    g_attn = norm_attn_g[i][None, :]
    px = _inproj(x.reshape(b * s, d), m_x, g_attn, w_in_bf, tables,
                 rows_per_batch=s, seq=s, rope_cols=V_OFF, tm=1024, tn=1024)
    pc = _inproj(ctx.reshape(b * n_ctx, d), m_c, g_attn, w_in_bf[:, K_OFF:CONV_OFF], tables,
                 rows_per_batch=b * n_ctx, seq=s, rope_cols=0, tm=n_ctx, tn=1024)
    px3 = px.reshape(b, s, IN_COLS)
    pc3 = pc.reshape(b, n_ctx, CONV_OFF - K_OFF)

    lam_p = jnp.stack([lambda_q1[i], lambda_k1[i], lambda_q2[i], lambda_k2[i]])
    tk = 256
    skv = s + n_ctx
    qt = px3[..., Q_OFF:K_OFF].transpose(0, 2, 1)
    k_all = jnp.concatenate([px3[..., K_OFF:V_OFF], pc3[..., :V_OFF - K_OFF]], axis=1)
    v_all = jnp.concatenate([px3[..., V_OFF:CONV_OFF], pc3[..., V_OFF - K_OFF:]], axis=1)
    vt_all = v_all.reshape(b, skv // tk, tk, N_HEADS, V_DIM).transpose(0, 3, 1, 4, 2)
    vt_all = jnp.concatenate([vt_all, jnp.ones(vt_all.shape[:3] + (SUM_ROWS, tk), BF16)], axis=3)
    o = _attention(qt, k_all, vt_all, lam_p, attn_subln_g[i][None, :], lambda_init=lambda_init, tq=512,
                   unroll=32)
    cx = _conv(px3, conv_dw_w[i], conv_dw_b[i][None, :], conv_ln_g[i][None, :], conv_ln_b[i][None, :],
               ts=512)

    wr = jnp.zeros((d, LANES), F32)
    wr = wr.at[:, :N_EXPERTS].set(router_expert_w[i]).at[:, N_EXPERTS:N_EXPERTS + N_GROUPS].set(router_group_w[i])
    rb = jnp.zeros((1, LANES), F32)
    rb = rb.at[0, :N_EXPERTS].set(router_expert_b[i]).at[0, N_EXPERTS:N_EXPERTS + N_GROUPS].set(router_group_b[i])
    wr_hi = wr.astype(BF16)
    wr_lo = (wr - wr_hi.astype(F32)).astype(BF16)

    x1, t, route = _merge(o.reshape(b * s, ATTN_WIDTH), cx.reshape(b * s, CONV_CH), px, x.reshape(b * s, d),
                          m_x, norm_ffn_g[i][None, :],
                          w_attn_proj[i].astype(BF16), w_conv_proj[i].astype(BF16), w_out[i].astype(BF16),
                          wr_hi, wr_lo, rb, rows_per_batch=s, tm=256)

    src, tile_expert, n_used, pos = _dispatch_plan(route, MOE_TM)
    y_sorted = _experts(src, tile_expert, n_used, t,
                        expert_w1[i].astype(BF16), expert_w3[i].astype(BF16), expert_w2[i].astype(BF16),
                        tm=MOE_TM)
    out = _combine(pos, y_sorted, x1, route, m_x, final_norm_g[None, :], rows_per_batch=s, tm=MOE_TM)
    return out.reshape(b, s, d)
```

```python
import functools
import math

import jax
import jax.numpy as jnp
from jax import lax
from jax.experimental import pallas as pl
from jax.experimental.pallas import tpu as pltpu

D_MODEL = 2048
GRID_W = 64
N_HEADS = 16
HEAD_DIM = 64
V_DIM = 2 * HEAD_DIM
ATTN_WIDTH = N_HEADS * V_DIM
CONV_CH = 1024
CONV_WIDTH = 31
N_GROUPS = 4
EXPERTS_PER_GROUP = 8
N_EXPERTS = N_GROUPS * EXPERTS_PER_GROUP
EXPERT_FF = 512
ROPE_BASE = 10000.0
ROPE_AXIS_DIM = HEAD_DIM // 2
N_MOD = 6
EPS = 1e-6

Q_OFF = 0
K_OFF = Q_OFF + N_HEADS * 2 * HEAD_DIM
V_OFF = K_OFF + N_HEADS * 2 * HEAD_DIM
CONV_OFF = V_OFF + ATTN_WIDTH
GATE_OFF = CONV_OFF + 2 * CONV_CH
IN_COLS = GATE_OFF + 2 * D_MODEL

LANES = 128
SUBLANES = 8
CONV_HALO = 16
CONV_ROWS = 16
VMEM_LIMIT = 56 * 1024 * 1024

F32 = jnp.float32
BF16 = jnp.bfloat16


def _params(*sem):
    return pltpu.CompilerParams(dimension_semantics=sem, vmem_limit_bytes=VMEM_LIMIT)


def _ada_kernel(c_ref, w_ref, b_ref, o_ref):
    c = c_ref[...]
    sc = c * jax.nn.sigmoid(c)
    o_ref[...] = jnp.dot(sc, w_ref[...], preferred_element_type=F32,
                         precision=lax.Precision.HIGHEST) + b_ref[...]


def _ada(cond, w, b):
    tn = 1024
    n = w.shape[1]
    return pl.pallas_call(
        _ada_kernel,
        grid=(n // tn,),
        in_specs=[pl.BlockSpec((8, D_MODEL), lambda j: (0, 0)),
                  pl.BlockSpec((D_MODEL, tn), lambda j: (0, j)),
                  pl.BlockSpec((1, tn), lambda j: (0, j))],
        out_specs=pl.BlockSpec((8, tn), lambda j: (0, j)),
        out_shape=jax.ShapeDtypeStruct((8, n), F32),
        compiler_params=_params("parallel"),
        name="ada",
    )(cond, w, b)


def _inproj_kernel(x_ref, m_ref, g_ref, w_ref, cos_ref, sa_ref, sb_ref, o_ref, hx_ref,
                   *, rope_tiles, q_tiles, tn):
    j = pl.program_id(1)

    @pl.when(j == 0)
    def _():
        x = x_ref[...]
        ms = jnp.mean(x * x, axis=-1, keepdims=True)
        y = x * lax.rsqrt(ms + EPS) * g_ref[...]
        shift = m_ref[0, 0:1, :]
        scale = m_ref[0, 1:2, :]
        hx_ref[...] = (y * (1.0 + scale) + shift).astype(BF16)

    def project():
        return jnp.dot(hx_ref[...], w_ref[...], preferred_element_type=F32)

    if rope_tiles == 0:
        o_ref[...] = project().astype(o_ref.dtype)
        return

    @pl.when(j < rope_tiles)
    def _():
        acc = project()
        qs = jnp.where(j < q_tiles, HEAD_DIM ** -0.5 * math.log2(math.e), 1.0).astype(F32)
        cos = cos_ref[...] * qs
        sa = sa_ref[...] * qs
        sb = sb_ref[...] * qs
        for c in range(tn // LANES):
            a = acc[:, c * LANES:(c + 1) * LANES]
            r = (a * cos + pltpu.roll(a, LANES - ROPE_AXIS_DIM // 2, 1) * sa
                 + pltpu.roll(a, ROPE_AXIS_DIM // 2, 1) * sb)
            o_ref[:, c * LANES:(c + 1) * LANES] = r.astype(o_ref.dtype)

    @pl.when(j >= rope_tiles)
    def _():
        o_ref[...] = project().astype(o_ref.dtype)


def _inproj(x2d, m, g, w, tables, *, rows_per_batch, seq, rope_cols, tm, tn):
    n, c = x2d.shape[0], w.shape[1]
    tiles_per_batch = rows_per_batch // tm
    seq_tiles = max(seq // tm, 1)
    cos, sa, sb = tables
    kern = functools.partial(_inproj_kernel, rope_tiles=rope_cols // tn,
                             q_tiles=(K_OFF - Q_OFF) // tn, tn=tn)
    tab_spec = pl.BlockSpec((tm, LANES), lambda i, j: (i % seq_tiles, 0))
    return pl.pallas_call(
        kern,
        grid=(n // tm, c // tn),
        in_specs=[pl.BlockSpec((tm, D_MODEL), lambda i, j: (i, 0)),
                  pl.BlockSpec((1, N_MOD, D_MODEL), lambda i, j: (i // tiles_per_batch, 0, 0)),
                  pl.BlockSpec((1, D_MODEL), lambda i, j: (0, 0)),
                  pl.BlockSpec((D_MODEL, tn), lambda i, j: (0, j)),
                  tab_spec, tab_spec, tab_spec],
        out_specs=pl.BlockSpec((tm, tn), lambda i, j: (i, j)),
        out_shape=jax.ShapeDtypeStruct((n, c), BF16),
        scratch_shapes=[pltpu.VMEM((tm, D_MODEL), BF16)],
        compiler_params=_params("parallel", "arbitrary"),
        name="inproj",
    )(x2d, m, g, w, cos, sa, sb)


def _rope_tables(seq):
    rows = seq // GRID_W
    row_pos = jnp.repeat(jnp.arange(rows), GRID_W).astype(F32)
    col_pos = jnp.tile(jnp.arange(GRID_W), rows).astype(F32)
    inv = ROPE_BASE ** (-jnp.arange(0, ROPE_AXIS_DIM, 2, dtype=F32) / ROPE_AXIS_DIM)
    ang_r = row_pos[:, None] * inv
    ang_c = col_pos[:, None] * inv
    ang = jnp.concatenate([ang_r, ang_r, ang_c, ang_c] * (LANES // HEAD_DIM), axis=-1)
    cos, sin = jnp.cos(ang), jnp.sin(ang)
    first_half = (jnp.arange(LANES) % ROPE_AXIS_DIM) < ROPE_AXIS_DIM // 2
    sa = jnp.where(first_half, -sin, 0.0)
    sb = jnp.where(first_half, 0.0, sin)
    return cos, sa, sb


SUM_ROWS = 16


def _attn_kernel(lam_ref, g_ref, q_ref, k_ref, v_ref, kc_ref, vc_ref, o_ref,
                 vt_ref, qm_ref, s_ref, p_ref, al_ref, m_ref, acc_ref, *, tq, tk, nt, lambda_init):

    @pl.when(pl.program_id(2) == 0)
    def _():
        for j in range(nt):
            v = v_ref[0, j * tk:(j + 1) * tk, :] if j < nt - 1 else vc_ref[0]
            vt_ref[j, 0:V_DIM, :] = v.astype(F32).T.astype(BF16)
        vt_ref[:, V_DIM:, :] = jnp.ones((nt, SUM_ROWS, tk), BF16)

    qt = q_ref[0].astype(F32).T.astype(BF16)
    row = lax.broadcasted_iota(jnp.int32, qt.shape, 0)
    zero = jnp.zeros_like(qt)
    qm_ref[0] = jnp.where(row < HEAD_DIM, qt, zero)
    qm_ref[1] = jnp.where(row >= HEAD_DIM, qt, zero)
    m_ref[...] = jnp.full(m_ref.shape, -jnp.inf, F32)
    acc_ref[...] = jnp.zeros(acc_ref.shape, F32)
    al_ref[...] = jnp.ones(al_ref.shape, F32)
    p_ref[1] = jnp.zeros(p_ref.shape[1:], BF16)

    def scores(j, slot):
        k = k_ref[0, j * tk:(j + 1) * tk, :] if j < nt - 1 else kc_ref[0]
        for mp in range(2):
            s_ref[slot, mp] = jnp.dot(k, qm_ref[mp], preferred_element_type=F32).astype(BF16)

    def softmax(slot):
        for mp in range(2):
            s = s_ref[slot, mp]
            m_prev = m_ref[mp]
            m_new = jnp.maximum(m_prev, jnp.max(s, axis=0, keepdims=True).astype(F32))
            al_ref[slot, mp] = jnp.exp2(m_prev - m_new)
            m_ref[mp] = m_new
            p_ref[slot, mp] = jnp.exp2(s - m_new.astype(BF16))

    def pv(j, slot):
        vt = vt_ref[j]
        for mp in range(2):
            acc_ref[mp] = (al_ref[slot, mp] * acc_ref[mp]
                           + jnp.dot(vt, p_ref[slot, mp], preferred_element_type=F32))

    scores(0, 0)
    for j in range(nt):
        if j + 1 < nt:
            scores(j + 1, (j + 1) % 2)
        softmax(j % 2)
        pv(max(j - 1, 0), (j + 1) % 2)
    pv(nt - 1, (nt - 1) % 2)

    lam_p = lam_ref[...]
    lam = (jnp.exp(jnp.sum(lam_p[0:1] * lam_p[1:2], axis=1, keepdims=True))
           - jnp.exp(jnp.sum(lam_p[2:3] * lam_p[3:4], axis=1, keepdims=True)) + lambda_init)
    a0, a1 = acc_ref[0], acc_ref[1]
    ot = a0[:V_DIM] / a0[V_DIM:V_DIM + 1] - lam * (a1[:V_DIM] / a1[V_DIM:V_DIM + 1])
    ms = jnp.mean(ot * ot, axis=0, keepdims=True)
    ot = ot * lax.rsqrt(ms + EPS)
    o = ot.T * (g_ref[...] * (1.0 - lambda_init))
    o_ref[0] = o.astype(o_ref.dtype)


def _attention(px, pc, lam_p, g, *, lambda_init, tq):
    b, s, _ = px.shape
    tk = pc.shape[1]
    assert s % tk == 0
    nt = s // tk + 1
    vrows = V_DIM + SUM_ROWS
    kblk, vblk = K_OFF // LANES, V_OFF // LANES
    kern = functools.partial(_attn_kernel, tq=tq, tk=tk, nt=nt, lambda_init=lambda_init)
    return pl.pallas_call(
        kern,
        grid=(b, N_HEADS, s // tq),
        in_specs=[pl.BlockSpec((4, HEAD_DIM), lambda bb, h, i: (0, 0)),
                  pl.BlockSpec((1, V_DIM), lambda bb, h, i: (0, 0)),
                  pl.BlockSpec((1, tq, LANES), lambda bb, h, i: (bb, i, h)),
                  pl.BlockSpec((1, s, LANES), lambda bb, h, i: (bb, 0, kblk + h)),
                  pl.BlockSpec((1, s, LANES), lambda bb, h, i: (bb, 0, vblk + h)),
                  pl.BlockSpec((1, tk, LANES), lambda bb, h, i: (bb, 0, h)),
                  pl.BlockSpec((1, tk, LANES), lambda bb, h, i: (bb, 0, N_HEADS + h))],
        out_specs=pl.BlockSpec((1, tq, LANES), lambda bb, h, i: (bb, i, h)),
        out_shape=jax.ShapeDtypeStruct((b, s, ATTN_WIDTH), BF16),
        scratch_shapes=[pltpu.VMEM((nt, vrows, tk), BF16),
                        pltpu.VMEM((2, LANES, tq), BF16),
                        pltpu.VMEM((2, 2, tk, tq), BF16),
                        pltpu.VMEM((2, 2, tk, tq), BF16),
                        pltpu.VMEM((2, 2, 1, tq), F32),
                        pltpu.VMEM((2, 1, tq), F32),
                        pltpu.VMEM((2, vrows, tq), F32)],
        compiler_params=_params("parallel", "parallel", "arbitrary"),
        name="attn",
    )(lam_p, g, px, px, px, pc, pc)


def _conv_kernel(uc_ref, up_ref, un_ref, w_ref, b_ref, g_ref, be_ref, o_ref, h_ref, acc_ref, wb_ref,
                 *, ts, n_tiles):
    i = pl.program_id(1)

    def glu(u):
        u = u.astype(F32)
        return u[:, :CONV_CH] * jax.nn.sigmoid(u[:, CONV_CH:])

    h_ref[0, 0:CONV_HALO, :] = glu(up_ref[0]) * jnp.where(i > 0, 1.0, 0.0).astype(F32)
    h_ref[0, CONV_HALO:CONV_HALO + ts, :] = glu(uc_ref[0])
    h_ref[0, CONV_HALO + ts:, :] = glu(un_ref[0]) * jnp.where(i < n_tiles - 1, 1.0, 0.0).astype(F32)
    rows = ts + 2 * CONV_HALO - SUBLANES
    for r in range(1, SUBLANES):
        h_ref[r, 0:rows, :] = h_ref[0, r:r + rows, :]

    for j in range(CONV_WIDTH):
        wb_ref[j] = jnp.broadcast_to(w_ref[j:j + 1, :], (CONV_ROWS, CONV_CH))
    base = CONV_HALO - CONV_WIDTH // 2

    def chunk(c, carry):
        r0 = pl.multiple_of(c * CONV_ROWS, CONV_ROWS)
        acc = jnp.zeros((CONV_ROWS, CONV_CH), F32) + b_ref[...]
        for j in range(CONV_WIDTH):
            q, r = divmod(base + j, SUBLANES)
            acc = acc + h_ref[r, pl.ds(r0 + SUBLANES * q, CONV_ROWS), :] * wb_ref[j]
        acc_ref[pl.ds(r0, CONV_ROWS), :] = acc
        return carry

    lax.fori_loop(0, ts // CONV_ROWS, chunk, 0)
    acc = acc_ref[...]
    mu = jnp.mean(acc, axis=-1, keepdims=True)
    d = acc - mu
    var = jnp.mean(d * d, axis=-1, keepdims=True)
    y = d * lax.rsqrt(var + EPS) * g_ref[...] + be_ref[...]
    o_ref[0] = (y * jax.nn.sigmoid(y)).astype(o_ref.dtype)


def _conv(px, w, b, g, be, *, ts):
    bsz, s, _ = px.shape
    n_tiles = s // ts
    ublk = CONV_OFF // (2 * CONV_CH)
    hpt = ts // CONV_HALO
    n_halo = s // CONV_HALO
    kern = functools.partial(_conv_kernel, ts=ts, n_tiles=n_tiles)
    vec = pl.BlockSpec((1, CONV_CH), lambda bb, i: (0, 0))
    return pl.pallas_call(
        kern,
        grid=(bsz, n_tiles),
        in_specs=[pl.BlockSpec((1, ts, 2 * CONV_CH), lambda bb, i: (bb, i, ublk)),
                  pl.BlockSpec((1, CONV_HALO, 2 * CONV_CH),
                               lambda bb, i: (bb, jnp.maximum(i * hpt - 1, 0), ublk)),
                  pl.BlockSpec((1, CONV_HALO, 2 * CONV_CH),
                               lambda bb, i: (bb, jnp.minimum((i + 1) * hpt, n_halo - 1), ublk)),
                  pl.BlockSpec((CONV_WIDTH, CONV_CH), lambda bb, i: (0, 0)),
                  vec, vec, vec],
        out_specs=pl.BlockSpec((1, ts, CONV_CH), lambda bb, i: (bb, i, 0)),
        out_shape=jax.ShapeDtypeStruct((bsz, s, CONV_CH), BF16),
        scratch_shapes=[pltpu.VMEM((SUBLANES, ts + 2 * CONV_HALO, CONV_CH), F32),
                        pltpu.VMEM((ts, CONV_CH), F32),
                        pltpu.VMEM((CONV_WIDTH, CONV_ROWS, CONV_CH), F32)],
        compiler_params=_params("parallel", "parallel"),
        name="conv",
    )(px, px, px, w, b, g, be)


def _split_dot(t, w_hi, w_lo):
    t_hi = t.astype(BF16)
    t_lo = (t - t_hi.astype(F32)).astype(BF16)
    return (jnp.dot(t_hi, w_hi, preferred_element_type=F32)
            + jnp.dot(t_hi, w_lo, preferred_element_type=F32)
            + jnp.dot(t_lo, w_hi, preferred_element_type=F32))


def _route(logits):
    lane = lax.broadcasted_iota(jnp.int32, logits.shape, 1)
    big = jnp.int32(1 << 20)
    neg = jnp.float32(-jnp.inf)
    is_g = (lane >= N_EXPERTS) & (lane < N_EXPERTS + N_GROUPS)
    lg = jnp.where(is_g, logits, neg)
    pgu = jnp.exp(lg - jnp.max(lg, axis=1, keepdims=True))
    pg = pgu / jnp.sum(pgu, axis=1, keepdims=True)
    g_w = jnp.max(pg, axis=1, keepdims=True)
    g_idx = jnp.min(jnp.where(is_g & (pg == g_w), lane - N_EXPERTS, big), axis=1, keepdims=True)
    sel = (lane < N_EXPERTS) & ((lane // EXPERTS_PER_GROUP) == g_idx)
    le = jnp.where(sel, logits, neg)
    peu = jnp.exp(le - jnp.max(le, axis=1, keepdims=True))
    pe = peu / jnp.sum(peu, axis=1, keepdims=True)
    v1 = jnp.max(pe, axis=1, keepdims=True)
    i1 = jnp.min(jnp.where(sel & (pe == v1), lane, big), axis=1, keepdims=True)
    rest = sel & (lane != i1)
    pe2 = jnp.where(rest, pe, -1.0)
    v2 = jnp.max(pe2, axis=1, keepdims=True)
    i2 = jnp.min(jnp.where(rest & (pe2 == v2), lane, big), axis=1, keepdims=True)
    den = v1 + v2
    return (jnp.where(lane == 0, i1.astype(F32), 0.0) + jnp.where(lane == 1, i2.astype(F32), 0.0)
            + jnp.where(lane == 2, g_w * (v1 / den), 0.0) + jnp.where(lane == 3, g_w * (v2 / den), 0.0))


def _merge_kernel(o_ref, cx_ref, g0_ref, g1_ref, x_ref, m_ref, ng_ref, wap_ref, wcp_ref, wout_ref,
                  wrh_ref, wrl_ref, rb_ref, x1_ref, t_ref, route_ref):
    ya = jnp.dot(o_ref[...], wap_ref[...], preferred_element_type=F32)
    yc = jnp.dot(cx_ref[...], wcp_ref[...], preferred_element_type=F32)
    merged = (jax.nn.sigmoid(g0_ref[...].astype(F32)) * ya
              + jax.nn.sigmoid(g1_ref[...].astype(F32)) * yc)
    mix = jnp.dot(merged.astype(BF16), wout_ref[...], preferred_element_type=F32)
    x1 = x_ref[...] + m_ref[0, 2:3, :] * mix
    x1_ref[...] = x1
    ms = jnp.mean(x1 * x1, axis=-1, keepdims=True)
    y = x1 * lax.rsqrt(ms + EPS) * ng_ref[...]
    t = y * (1.0 + m_ref[0, 4:5, :]) + m_ref[0, 3:4, :]
    t_ref[...] = t
    logits = _split_dot(t, wrh_ref[...], wrl_ref[...]) + rb_ref[...]
    route_ref[...] = _route(logits)


def _merge(o2d, cx2d, px2d, x2d, m, ng, wap, wcp, wout, wr_hi, wr_lo, rb, *, rows_per_batch, tm):
    n = x2d.shape[0]
    tiles_per_batch = rows_per_batch // tm
    gblk = GATE_OFF // D_MODEL
    row = lambda i: (i, 0)
    const = lambda i: (0, 0)

    def resident(shape):
        return pl.BlockSpec(shape, const, pipeline_mode=pl.Buffered(1))

    return pl.pallas_call(
        _merge_kernel,
        grid=(n // tm,),
        in_specs=[pl.BlockSpec((tm, ATTN_WIDTH), row),
                  pl.BlockSpec((tm, CONV_CH), row),
                  pl.BlockSpec((tm, D_MODEL), lambda i: (i, gblk)),
                  pl.BlockSpec((tm, D_MODEL), lambda i: (i, gblk + 1)),
                  pl.BlockSpec((tm, D_MODEL), row),
                  pl.BlockSpec((1, N_MOD, D_MODEL), lambda i: (i // tiles_per_batch, 0, 0)),
                  pl.BlockSpec((1, D_MODEL), const),
                  resident((ATTN_WIDTH, D_MODEL)),
                  resident((CONV_CH, D_MODEL)),
                  resident((D_MODEL, D_MODEL)),
                  resident((D_MODEL, LANES)),
                  resident((D_MODEL, LANES)),
                  pl.BlockSpec((1, LANES), const)],
        out_specs=[pl.BlockSpec((tm, D_MODEL), row),
                   pl.BlockSpec((tm, D_MODEL), row),
                   pl.BlockSpec((tm, LANES), row)],
        out_shape=[jax.ShapeDtypeStruct((n, D_MODEL), F32),
                   jax.ShapeDtypeStruct((n, D_MODEL), F32),
                   jax.ShapeDtypeStruct((n, LANES), F32)],
        compiler_params=_params("parallel"),
        name="merge",
    )(o2d, cx2d, px2d, px2d, x2d, m, ng, wap, wcp, wout, wr_hi, wr_lo, rb)


MOE_TM = 256


def _dispatch_plan(route, tm):
    n = route.shape[0]
    n_tiles = (2 * n) // tm + N_EXPERTS
    flat = route[:, 0:2].astype(jnp.int32).reshape(-1)
    onehot = (flat[:, None] == jnp.arange(N_EXPERTS, dtype=jnp.int32)[None, :]).astype(jnp.int32)
    blk = LANES
    tri = (jnp.arange(blk)[:, None] >= jnp.arange(blk)[None, :]).astype(BF16)
    within = jnp.einsum('ij,bjk->bik', tri, onehot.reshape(-1, blk, N_EXPERTS).astype(BF16),
                        preferred_element_type=F32).astype(jnp.int32)
    totals = within[:, -1, :]
    csum = (within + (jnp.cumsum(totals, axis=0) - totals)[:, None, :]).reshape(2 * n, N_EXPERTS)
    rank = jnp.sum((csum - onehot) * onehot, axis=1)
    padded = ((csum[-1] + tm - 1) // tm) * tm
    ends = jnp.cumsum(padded)
    dest = (ends - padded)[flat] + rank
    src = jnp.zeros((n_tiles * tm,), jnp.int32).at[dest].set(jnp.arange(2 * n, dtype=jnp.int32) // 2)
    tile_start = jnp.arange(n_tiles, dtype=jnp.int32) * tm
    tile_expert = jnp.minimum(jnp.sum((tile_start[:, None] >= ends[None, :]).astype(jnp.int32), axis=1),
                              N_EXPERTS - 1)
    n_used = (ends[-1] // tm).reshape(1)
    pos = dest.reshape(n, 2)
    return src.reshape(n_tiles, 1, tm), tile_expert, n_used, pos


def _row_gather(idx_ref, src_hbm, dst_ref, sem, tm):
    for r in range(tm):
        pltpu.make_async_copy(src_hbm.at[pl.ds(idx_ref[0, 0, r], 1)], dst_ref.at[pl.ds(r, 1)], sem).start()


def _row_gather_wait(src_hbm, dst_ref, sem, tm):
    for r in range(tm):
        pltpu.make_async_copy(src_hbm.at[pl.ds(0, 1)], dst_ref.at[pl.ds(r, 1)], sem).wait()


def _expert_kernel(te_ref, nu_ref, src_ref, srcn_ref, t_hbm, w1_ref, w3_ref, w2_ref, y_ref, xbuf, sem, *, tm):
    t = pl.program_id(0)
    nu = nu_ref[0]
    slot = t % 2

    @pl.when(t == 0)
    def _():
        _row_gather(src_ref, t_hbm, xbuf.at[0], sem.at[0], tm)

    @pl.when(t < nu)
    def _():
        _row_gather(srcn_ref, t_hbm, xbuf.at[1 - slot], sem.at[1 - slot], tm)
        _row_gather_wait(t_hbm, xbuf.at[slot], sem.at[slot], tm)
        x = xbuf[slot].astype(BF16)
        h1 = jnp.dot(x, w1_ref[0], preferred_element_type=F32)
        h3 = jnp.dot(x, w3_ref[0], preferred_element_type=F32)
        hid = (h1 * jax.nn.sigmoid(h1)) * h3
        y_ref[...] = jnp.dot(hid.astype(BF16), w2_ref[0], preferred_element_type=F32)

    @pl.when(t == nu - 1)
    def _():
        _row_gather_wait(t_hbm, xbuf.at[1 - slot], sem.at[1 - slot], tm)

    @pl.when(t >= nu)
    def _():
        y_ref[...] = jnp.zeros(y_ref.shape, F32)


def _experts(src, tile_expert, n_used, t2d, w1, w3, w2, *, tm):
    n_tiles = src.shape[0]
    wspec = lambda shape: pl.BlockSpec(shape, lambda t, te, nu: (te[t], 0, 0))
    idx = lambda f: pl.BlockSpec((1, 1, tm), f, memory_space=pltpu.SMEM)
    grid_spec = pltpu.PrefetchScalarGridSpec(
        num_scalar_prefetch=2,
        grid=(n_tiles,),
        in_specs=[idx(lambda t, te, nu: (t, 0, 0)),
                  idx(lambda t, te, nu: (jnp.minimum(t + 1, n_tiles - 1), 0, 0)),
                  pl.BlockSpec(memory_space=pl.ANY),
                  wspec((1, D_MODEL, EXPERT_FF)), wspec((1, D_MODEL, EXPERT_FF)),
                  wspec((1, EXPERT_FF, D_MODEL))],
        out_specs=pl.BlockSpec((tm, D_MODEL), lambda t, te, nu: (t, 0)),
        scratch_shapes=[pltpu.VMEM((2, tm, D_MODEL), F32), pltpu.SemaphoreType.DMA((2,))])
    return pl.pallas_call(
        functools.partial(_expert_kernel, tm=tm),
        grid_spec=grid_spec,
        out_shape=jax.ShapeDtypeStruct((n_tiles * tm, D_MODEL), F32),
        compiler_params=_params("arbitrary"),
        name="experts",
    )(tile_expert, n_used, src, src, t2d, w1, w3, w2)


def _combine_kernel(pa_ref, pb_ref, pan_ref, pbn_ref, y_hbm, x1_ref, route_ref, m_ref, fg_ref, o_ref,
                    ybuf, sem, *, tm, n_tiles):
    i = pl.program_id(0)
    slot = i % 2

    def gather(ia_ref, ib_ref, sl):
        _row_gather(ia_ref, y_hbm, ybuf.at[sl, 0], sem.at[sl], tm)
        _row_gather(ib_ref, y_hbm, ybuf.at[sl, 1], sem.at[sl], tm)

    @pl.when(i == 0)
    def _():
        gather(pa_ref, pb_ref, 0)

    @pl.when(i + 1 < n_tiles)
    def _():
        gather(pan_ref, pbn_ref, 1 - slot)

    _row_gather_wait(y_hbm, ybuf.at[slot, 0], sem.at[slot], tm)
    _row_gather_wait(y_hbm, ybuf.at[slot, 1], sem.at[slot], tm)
    route = route_ref[...]
    y = route[:, 2:3] * ybuf[slot, 0] + route[:, 3:4] * ybuf[slot, 1]
    x2 = x1_ref[...] + m_ref[0, 5:6, :] * y
    ms = jnp.mean(x2 * x2, axis=-1, keepdims=True)
    o_ref[...] = x2 * lax.rsqrt(ms + EPS) * fg_ref[...]


def _combine(pos, y_sorted, x1, route, m, fg, *, rows_per_batch, tm):
    n = x1.shape[0]
    n_tiles = n // tm
    tiles_per_batch = rows_per_batch // tm
    pos_a = pos[:, 0].reshape(n_tiles, 1, tm)
    pos_b = pos[:, 1].reshape(n_tiles, 1, tm)
    cur = lambda i: (i, 0, 0)
    nxt = lambda i: (jnp.minimum(i + 1, n_tiles - 1), 0, 0)
    idx = lambda f: pl.BlockSpec((1, 1, tm), f, memory_space=pltpu.SMEM)
    row = lambda i: (i, 0)
    return pl.pallas_call(
        functools.partial(_combine_kernel, tm=tm, n_tiles=n_tiles),
        grid=(n_tiles,),
        in_specs=[idx(cur), idx(cur), idx(nxt), idx(nxt),
                  pl.BlockSpec(memory_space=pl.ANY),
                  pl.BlockSpec((tm, D_MODEL), row),
                  pl.BlockSpec((tm, LANES), row),
                  pl.BlockSpec((1, N_MOD, D_MODEL), lambda i: (i // tiles_per_batch, 0, 0)),
                  pl.BlockSpec((1, D_MODEL), lambda i: (0, 0))],
        out_specs=pl.BlockSpec((tm, D_MODEL), row),
        out_shape=jax.ShapeDtypeStruct((n, D_MODEL), F32),
        scratch_shapes=[pltpu.VMEM((2, 2, tm, D_MODEL), F32), pltpu.SemaphoreType.DMA((2,))],
        compiler_params=_params("arbitrary"),
        name="combine",
    )(pos_a, pos_b, pos_a, pos_b, y_sorted, x1, route, m, fg)


def kernel(x, c, ctx, c_ctx, w_ada, b_ada, norm_attn_g, norm_ffn_g, w_in, lambda_q1, lambda_k1, lambda_q2, lambda_k2, attn_subln_g, conv_dw_w, conv_dw_b, conv_ln_g, conv_ln_b, w_attn_proj, w_conv_proj, w_out, router_group_w, router_group_b, router_expert_w, router_expert_b, expert_w1, expert_w3, expert_w2, final_norm_g):
    b, s, d = x.shape
    n_ctx = ctx.shape[1]
    depth = w_ada.shape[0]
    assert depth == 1 and d == D_MODEL and b <= 7
    i = 0
    lambda_init = 0.8 - 0.6 * math.exp(-0.3 * i)

    cond = jnp.zeros((8, d), F32).at[:b].set(c).at[b].set(c_ctx)
    mods = _ada(cond, w_ada[i], b_ada[i][None, :])
    m_x = mods[:b].reshape(b, N_MOD, d)
    m_c = mods[b:b + 1].reshape(1, N_MOD, d)

    w_in_bf = w_in[i].astype(BF16)
    tables = _rope_tables(s)
    g_attn = norm_attn_g[i][None, :]
    px = _inproj(x.reshape(b * s, d), m_x, g_attn, w_in_bf, tables,
                 rows_per_batch=s, seq=s, rope_cols=V_OFF, tm=1024, tn=1024)
    pc = _inproj(ctx.reshape(b * n_ctx, d), m_c, g_attn, w_in_bf[:, K_OFF:CONV_OFF], tables,
                 rows_per_batch=b * n_ctx, seq=s, rope_cols=0, tm=n_ctx, tn=1024)
    px3 = px.reshape(b, s, IN_COLS)
    pc3 = pc.reshape(b, n_ctx, CONV_OFF - K_OFF)

    lam_p = jnp.stack([lambda_q1[i], lambda_k1[i], lambda_q2[i], lambda_k2[i]])
    o = _attention(px3, pc3, lam_p, attn_subln_g[i][None, :], lambda_init=lambda_init, tq=256)
    cx = _conv(px3, conv_dw_w[i], conv_dw_b[i][None, :], conv_ln_g[i][None, :], conv_ln_b[i][None, :],
               ts=512)

    wr = jnp.zeros((d, LANES), F32)
    wr = wr.at[:, :N_EXPERTS].set(router_expert_w[i]).at[:, N_EXPERTS:N_EXPERTS + N_GROUPS].set(router_group_w[i])
    rb = jnp.zeros((1, LANES), F32)
    rb = rb.at[0, :N_EXPERTS].set(router_expert_b[i]).at[0, N_EXPERTS:N_EXPERTS + N_GROUPS].set(router_group_b[i])
    wr_hi = wr.astype(BF16)
    wr_lo = (wr - wr_hi.astype(F32)).astype(BF16)

    x1, t, route = _merge(o.reshape(b * s, ATTN_WIDTH), cx.reshape(b * s, CONV_CH), px, x.reshape(b * s, d),
                          m_x, norm_ffn_g[i][None, :],
                          w_attn_proj[i].astype(BF16), w_conv_proj[i].astype(BF16), w_out[i].astype(BF16),
                          wr_hi, wr_lo, rb, rows_per_batch=s, tm=256)

    src, tile_expert, n_used, pos = _dispatch_plan(route, MOE_TM)
    y_sorted = _experts(src, tile_expert, n_used, t,
                        expert_w1[i].astype(BF16), expert_w3[i].astype(BF16), expert_w2[i].astype(BF16),
                        tm=MOE_TM)
    out = _combine(pos, y_sorted, x1, route, m_x, final_norm_g[None, :], rows_per_batch=s, tm=MOE_TM)
    return out.reshape(b, s, d)
```

```python
import functools
import math

import jax
import jax.numpy as jnp
from jax import lax
from jax.experimental import pallas as pl
from jax.experimental.pallas import tpu as pltpu

D_MODEL = 2048
GRID_W = 64
N_HEADS = 16
HEAD_DIM = 64
V_DIM = 2 * HEAD_DIM
ATTN_WIDTH = N_HEADS * V_DIM
CONV_CH = 1024
CONV_WIDTH = 31
N_GROUPS = 4
EXPERTS_PER_GROUP = 8
N_EXPERTS = N_GROUPS * EXPERTS_PER_GROUP
EXPERT_FF = 512
ROPE_BASE = 10000.0
ROPE_AXIS_DIM = HEAD_DIM // 2
N_MOD = 6
EPS = 1e-6

Q_OFF = 0
K_OFF = Q_OFF + N_HEADS * 2 * HEAD_DIM
V_OFF = K_OFF + N_HEADS * 2 * HEAD_DIM
CONV_OFF = V_OFF + ATTN_WIDTH
GATE_OFF = CONV_OFF + 2 * CONV_CH
IN_COLS = GATE_OFF + 2 * D_MODEL

LANES = 128
SUBLANES = 8
CONV_HALO = 16
CONV_ROWS = 16
VMEM_LIMIT = 56 * 1024 * 1024

F32 = jnp.float32
BF16 = jnp.bfloat16


def _params(*sem):
    return pltpu.CompilerParams(dimension_semantics=sem, vmem_limit_bytes=VMEM_LIMIT)


def _ada_kernel(c_ref, w_ref, b_ref, o_ref):
    c = c_ref[...]
    sc = c * jax.nn.sigmoid(c)
    o_ref[...] = jnp.dot(sc, w_ref[...], preferred_element_type=F32,
                         precision=lax.Precision.HIGHEST) + b_ref[...]


def _ada(cond, w, b):
    tn = 1024
    n = w.shape[1]
    return pl.pallas_call(
        _ada_kernel,
        grid=(n // tn,),
        in_specs=[pl.BlockSpec((8, D_MODEL), lambda j: (0, 0)),
                  pl.BlockSpec((D_MODEL, tn), lambda j: (0, j)),
                  pl.BlockSpec((1, tn), lambda j: (0, j))],
        out_specs=pl.BlockSpec((8, tn), lambda j: (0, j)),
        out_shape=jax.ShapeDtypeStruct((8, n), F32),
        compiler_params=_params("parallel"),
        name="ada",
    )(cond, w, b)


def _inproj_kernel(x_ref, m_ref, g_ref, w_ref, cos_ref, sa_ref, sb_ref, o_ref, hx_ref,
                   *, rope_tiles, q_tiles, tn):
    j = pl.program_id(1)

    @pl.when(j == 0)
    def _():
        x = x_ref[...]
        ms = jnp.mean(x * x, axis=-1, keepdims=True)
        y = x * lax.rsqrt(ms + EPS) * g_ref[...]
        shift = m_ref[0, 0:1, :]
        scale = m_ref[0, 1:2, :]
        hx_ref[...] = (y * (1.0 + scale) + shift).astype(BF16)

    def project():
        return jnp.dot(hx_ref[...], w_ref[...], preferred_element_type=F32)

    if rope_tiles == 0:
        o_ref[...] = project().astype(o_ref.dtype)
        return

    @pl.when(j < rope_tiles)
    def _():
        acc = project()
        qs = jnp.where(j < q_tiles, HEAD_DIM ** -0.5 * math.log2(math.e), 1.0).astype(F32)
        cos = cos_ref[...] * qs
        sa = sa_ref[...] * qs
        sb = sb_ref[...] * qs
        for c in range(tn // LANES):
            a = acc[:, c * LANES:(c + 1) * LANES]
            r = (a * cos + pltpu.roll(a, LANES - ROPE_AXIS_DIM // 2, 1) * sa
                 + pltpu.roll(a, ROPE_AXIS_DIM // 2, 1) * sb)
            o_ref[:, c * LANES:(c + 1) * LANES] = r.astype(o_ref.dtype)

    @pl.when(j >= rope_tiles)
    def _():
        o_ref[...] = project().astype(o_ref.dtype)


def _inproj(x2d, m, g, w, tables, *, rows_per_batch, seq, rope_cols, tm, tn):
    n, c = x2d.shape[0], w.shape[1]
    tiles_per_batch = rows_per_batch // tm
    seq_tiles = max(seq // tm, 1)
    cos, sa, sb = tables
    kern = functools.partial(_inproj_kernel, rope_tiles=rope_cols // tn,
                             q_tiles=(K_OFF - Q_OFF) // tn, tn=tn)
    tab_spec = pl.BlockSpec((tm, LANES), lambda i, j: (i % seq_tiles, 0))
    return pl.pallas_call(
        kern,
        grid=(n // tm, c // tn),
        in_specs=[pl.BlockSpec((tm, D_MODEL), lambda i, j: (i, 0)),
                  pl.BlockSpec((1, N_MOD, D_MODEL), lambda i, j: (i // tiles_per_batch, 0, 0)),
                  pl.BlockSpec((1, D_MODEL), lambda i, j: (0, 0)),
                  pl.BlockSpec((D_MODEL, tn), lambda i, j: (0, j)),
                  tab_spec, tab_spec, tab_spec],
        out_specs=pl.BlockSpec((tm, tn), lambda i, j: (i, j)),
        out_shape=jax.ShapeDtypeStruct((n, c), BF16),
        scratch_shapes=[pltpu.VMEM((tm, D_MODEL), BF16)],
        compiler_params=_params("parallel", "arbitrary"),
        name="inproj",
    )(x2d, m, g, w, cos, sa, sb)


def _rope_tables(seq):
    rows = seq // GRID_W
    row_pos = jnp.repeat(jnp.arange(rows), GRID_W).astype(F32)
    col_pos = jnp.tile(jnp.arange(GRID_W), rows).astype(F32)
    inv = ROPE_BASE ** (-jnp.arange(0, ROPE_AXIS_DIM, 2, dtype=F32) / ROPE_AXIS_DIM)
    ang_r = row_pos[:, None] * inv
    ang_c = col_pos[:, None] * inv
    ang = jnp.concatenate([ang_r, ang_r, ang_c, ang_c] * (LANES // HEAD_DIM), axis=-1)
    cos, sin = jnp.cos(ang), jnp.sin(ang)
    first_half = (jnp.arange(LANES) % ROPE_AXIS_DIM) < ROPE_AXIS_DIM // 2
    sa = jnp.where(first_half, -sin, 0.0)
    sb = jnp.where(first_half, 0.0, sin)
    return cos, sa, sb


SUM_ROWS = 16


def _attn_kernel(lam_ref, g_ref, q_ref, k_ref, v_ref, kc_ref, vc_ref, o_ref,
                 vt_ref, qm_ref, s_ref, p_ref, al_ref, m_ref, acc_ref, *, tq, tk, nt, lambda_init):

    @pl.when(pl.program_id(2) == 0)
    def _():
        for j in range(nt):
            v = v_ref[0, j * tk:(j + 1) * tk, :] if j < nt - 1 else vc_ref[0]
            vt_ref[j, 0:V_DIM, :] = v.astype(F32).T.astype(BF16)
        vt_ref[:, V_DIM:, :] = jnp.ones((nt, SUM_ROWS, tk), BF16)

    qt = q_ref[0].astype(F32).T.astype(BF16)
    row = lax.broadcasted_iota(jnp.int32, qt.shape, 0)
    zero = jnp.zeros_like(qt)
    qm_ref[0] = jnp.where(row < HEAD_DIM, qt, zero)
    qm_ref[1] = jnp.where(row >= HEAD_DIM, qt, zero)
    m_ref[...] = jnp.full(m_ref.shape, -jnp.inf, F32)
    acc_ref[...] = jnp.zeros(acc_ref.shape, F32)
    al_ref[...] = jnp.ones(al_ref.shape, F32)
    p_ref[1] = jnp.zeros(p_ref.shape[1:], BF16)

    def scores(j, slot):
        k = k_ref[0, j * tk:(j + 1) * tk, :] if j < nt - 1 else kc_ref[0]
        for mp in range(2):
            s_ref[slot, mp] = jnp.dot(k, qm_ref[mp], preferred_element_type=F32).astype(BF16)

    def softmax(slot):
        for mp in range(2):
            s = s_ref[slot, mp]
            m_prev = m_ref[mp]
            m_new = jnp.maximum(m_prev, jnp.max(s, axis=0, keepdims=True).astype(F32))
            al_ref[slot, mp] = jnp.exp2(m_prev - m_new)
            m_ref[mp] = m_new
            p_ref[slot, mp] = jnp.exp2(s - m_new.astype(BF16))

    def pv(j, slot):
        vt = vt_ref[j]
        for mp in range(2):
            acc_ref[mp] = (al_ref[slot, mp] * acc_ref[mp]
                           + jnp.dot(vt, p_ref[slot, mp], preferred_element_type=F32))

    scores(0, 0)
    for j in range(nt):
        if j + 1 < nt:
            scores(j + 1, (j + 1) % 2)
        softmax(j % 2)
        pv(max(j - 1, 0), (j + 1) % 2)
    pv(nt - 1, (nt - 1) % 2)

    lam_p = lam_ref[...]
    lam = (jnp.exp(jnp.sum(lam_p[0:1] * lam_p[1:2], axis=1, keepdims=True))
           - jnp.exp(jnp.sum(lam_p[2:3] * lam_p[3:4], axis=1, keepdims=True)) + lambda_init)
    a0, a1 = acc_ref[0], acc_ref[1]
    ot = a0[:V_DIM] / a0[V_DIM:V_DIM + 1] - lam * (a1[:V_DIM] / a1[V_DIM:V_DIM + 1])
    ms = jnp.mean(ot * ot, axis=0, keepdims=True)
    ot = ot * lax.rsqrt(ms + EPS)
    o = ot.T * (g_ref[...] * (1.0 - lambda_init))
    o_ref[0] = o.astype(o_ref.dtype)


def _attention(px, pc, lam_p, g, *, lambda_init, tq):
    b, s, _ = px.shape
    tk = pc.shape[1]
    assert s % tk == 0
    nt = s // tk + 1
    vrows = V_DIM + SUM_ROWS
    kblk, vblk = K_OFF // LANES, V_OFF // LANES
    kern = functools.partial(_attn_kernel, tq=tq, tk=tk, nt=nt, lambda_init=lambda_init)
    return pl.pallas_call(
        kern,
        grid=(b, N_HEADS, s // tq),
        in_specs=[pl.BlockSpec((4, HEAD_DIM), lambda bb, h, i: (0, 0)),
                  pl.BlockSpec((1, V_DIM), lambda bb, h, i: (0, 0)),
                  pl.BlockSpec((1, tq, LANES), lambda bb, h, i: (bb, i, h)),
                  pl.BlockSpec((1, s, LANES), lambda bb, h, i: (bb, 0, kblk + h)),
                  pl.BlockSpec((1, s, LANES), lambda bb, h, i: (bb, 0, vblk + h)),
                  pl.BlockSpec((1, tk, LANES), lambda bb, h, i: (bb, 0, h)),
                  pl.BlockSpec((1, tk, LANES), lambda bb, h, i: (bb, 0, N_HEADS + h))],
        out_specs=pl.BlockSpec((1, tq, LANES), lambda bb, h, i: (bb, i, h)),
        out_shape=jax.ShapeDtypeStruct((b, s, ATTN_WIDTH), BF16),
        scratch_shapes=[pltpu.VMEM((nt, vrows, tk), BF16),
                        pltpu.VMEM((2, LANES, tq), BF16),
                        pltpu.VMEM((2, 2, tk, tq), BF16),
                        pltpu.VMEM((2, 2, tk, tq), BF16),
                        pltpu.VMEM((2, 2, 1, tq), F32),
                        pltpu.VMEM((2, 1, tq), F32),
                        pltpu.VMEM((2, vrows, tq), F32)],
        compiler_params=_params("parallel", "parallel", "arbitrary"),
        name="attn",
    )(lam_p, g, px, px, px, pc, pc)


def _conv_kernel(uc_ref, up_ref, un_ref, w_ref, b_ref, g_ref, be_ref, o_ref, h_ref, acc_ref, wb_ref,
                 *, ts, n_tiles):
    i = pl.program_id(1)

    def glu(u):
        u = u.astype(F32)
        return u[:, :CONV_CH] * jax.nn.sigmoid(u[:, CONV_CH:])

    h_ref[0, 0:CONV_HALO, :] = glu(up_ref[0]) * jnp.where(i > 0, 1.0, 0.0).astype(F32)
    h_ref[0, CONV_HALO:CONV_HALO + ts, :] = glu(uc_ref[0])
    h_ref[0, CONV_HALO + ts:, :] = glu(un_ref[0]) * jnp.where(i < n_tiles - 1, 1.0, 0.0).astype(F32)
    rows = ts + 2 * CONV_HALO - SUBLANES
    for r in range(1, SUBLANES):
        h_ref[r, 0:rows, :] = h_ref[0, r:r + rows, :]

    for j in range(CONV_WIDTH):
        wb_ref[j] = jnp.broadcast_to(w_ref[j:j + 1, :], (CONV_ROWS, CONV_CH))
    base = CONV_HALO - CONV_WIDTH // 2

    def chunk(c, carry):
        r0 = pl.multiple_of(c * CONV_ROWS, CONV_ROWS)
        acc = jnp.zeros((CONV_ROWS, CONV_CH), F32) + b_ref[...]
        for j in range(CONV_WIDTH):
            q, r = divmod(base + j, SUBLANES)
            acc = acc + h_ref[r, pl.ds(r0 + SUBLANES * q, CONV_ROWS), :] * wb_ref[j]
        acc_ref[pl.ds(r0, CONV_ROWS), :] = acc
        return carry

    lax.fori_loop(0, ts // CONV_ROWS, chunk, 0)
    acc = acc_ref[...]
    mu = jnp.mean(acc, axis=-1, keepdims=True)
    d = acc - mu
    var = jnp.mean(d * d, axis=-1, keepdims=True)
    y = d * lax.rsqrt(var + EPS) * g_ref[...] + be_ref[...]
    o_ref[0] = (y * jax.nn.sigmoid(y)).astype(o_ref.dtype)


def _conv(px, w, b, g, be, *, ts):
    bsz, s, _ = px.shape
    n_tiles = s // ts
    ublk = CONV_OFF // (2 * CONV_CH)
    hpt = ts // CONV_HALO
    n_halo = s // CONV_HALO
    kern = functools.partial(_conv_kernel, ts=ts, n_tiles=n_tiles)
    vec = pl.BlockSpec((1, CONV_CH), lambda bb, i: (0, 0))
    return pl.pallas_call(
        kern,
        grid=(bsz, n_tiles),
        in_specs=[pl.BlockSpec((1, ts, 2 * CONV_CH), lambda bb, i: (bb, i, ublk)),
                  pl.BlockSpec((1, CONV_HALO, 2 * CONV_CH),
                               lambda bb, i: (bb, jnp.maximum(i * hpt - 1, 0), ublk)),
                  pl.BlockSpec((1, CONV_HALO, 2 * CONV_CH),
                               lambda bb, i: (bb, jnp.minimum((i + 1) * hpt, n_halo - 1), ublk)),
                  pl.BlockSpec((CONV_WIDTH, CONV_CH), lambda bb, i: (0, 0)),
                  vec, vec, vec],
        out_specs=pl.BlockSpec((1, ts, CONV_CH), lambda bb, i: (bb, i, 0)),
        out_shape=jax.ShapeDtypeStruct((bsz, s, CONV_CH), BF16),
        scratch_shapes=[pltpu.VMEM((SUBLANES, ts + 2 * CONV_HALO, CONV_CH), F32),
                        pltpu.VMEM((ts, CONV_CH), F32),
                        pltpu.VMEM((CONV_WIDTH, CONV_ROWS, CONV_CH), F32)],
        compiler_params=_params("parallel", "parallel"),
        name="conv",
    )(px, px, px, w, b, g, be)


def _split_dot(t, w_hi, w_lo):
    t_hi = t.astype(BF16)
    t_lo = (t - t_hi.astype(F32)).astype(BF16)
    return (jnp.dot(t_hi, w_hi, preferred_element_type=F32)
            + jnp.dot(t_hi, w_lo, preferred_element_type=F32)
            + jnp.dot(t_lo, w_hi, preferred_element_type=F32))


def _route(logits):
    lane = lax.broadcasted_iota(jnp.int32, logits.shape, 1)
    big = jnp.int32(1 << 20)
    neg = jnp.float32(-jnp.inf)
    is_g = (lane >= N_EXPERTS) & (lane < N_EXPERTS + N_GROUPS)
    lg = jnp.where(is_g, logits, neg)
    pgu = jnp.exp(lg - jnp.max(lg, axis=1, keepdims=True))
    pg = pgu / jnp.sum(pgu, axis=1, keepdims=True)
    g_w = jnp.max(pg, axis=1, keepdims=True)
    g_idx = jnp.min(jnp.where(is_g & (pg == g_w), lane - N_EXPERTS, big), axis=1, keepdims=True)
    sel = (lane < N_EXPERTS) & ((lane // EXPERTS_PER_GROUP) == g_idx)
    le = jnp.where(sel, logits, neg)
    peu = jnp.exp(le - jnp.max(le, axis=1, keepdims=True))
    pe = peu / jnp.sum(peu, axis=1, keepdims=True)
    v1 = jnp.max(pe, axis=1, keepdims=True)
    i1 = jnp.min(jnp.where(sel & (pe == v1), lane, big), axis=1, keepdims=True)
    rest = sel & (lane != i1)
    pe2 = jnp.where(rest, pe, -1.0)
    v2 = jnp.max(pe2, axis=1, keepdims=True)
    i2 = jnp.min(jnp.where(rest & (pe2 == v2), lane, big), axis=1, keepdims=True)
    den = v1 + v2
    return (jnp.where(lane == 0, i1.astype(F32), 0.0) + jnp.where(lane == 1, i2.astype(F32), 0.0)
            + jnp.where(lane == 2, g_w * (v1 / den), 0.0) + jnp.where(lane == 3, g_w * (v2 / den), 0.0))


def _merge_kernel(o_ref, cx_ref, g0_ref, g1_ref, x_ref, m_ref, ng_ref, wap_ref, wcp_ref, wout_ref,
                  wrh_ref, wrl_ref, rb_ref, x1_ref, t_ref, route_ref):
    ya = jnp.dot(o_ref[...], wap_ref[...], preferred_element_type=F32)
    yc = jnp.dot(cx_ref[...], wcp_ref[...], preferred_element_type=F32)
    merged = (jax.nn.sigmoid(g0_ref[...].astype(F32)) * ya
              + jax.nn.sigmoid(g1_ref[...].astype(F32)) * yc)
    mix = jnp.dot(merged.astype(BF16), wout_ref[...], preferred_element_type=F32)
    x1 = x_ref[...] + m_ref[0, 2:3, :] * mix
    x1_ref[...] = x1
    ms = jnp.mean(x1 * x1, axis=-1, keepdims=True)
    y = x1 * lax.rsqrt(ms + EPS) * ng_ref[...]
    t = y * (1.0 + m_ref[0, 4:5, :]) + m_ref[0, 3:4, :]
    t_ref[...] = t
    logits = _split_dot(t, wrh_ref[...], wrl_ref[...]) + rb_ref[...]
    route_ref[...] = _route(logits)


def _merge(o2d, cx2d, px2d, x2d, m, ng, wap, wcp, wout, wr_hi, wr_lo, rb, *, rows_per_batch, tm):
    n = x2d.shape[0]
    tiles_per_batch = rows_per_batch // tm
    gblk = GATE_OFF // D_MODEL
    row = lambda i: (i, 0)
    const = lambda i: (0, 0)

    def resident(shape):
        return pl.BlockSpec(shape, const, pipeline_mode=pl.Buffered(1))

    return pl.pallas_call(
        _merge_kernel,
        grid=(n // tm,),
        in_specs=[pl.BlockSpec((tm, ATTN_WIDTH), row),
                  pl.BlockSpec((tm, CONV_CH), row),
                  pl.BlockSpec((tm, D_MODEL), lambda i: (i, gblk)),
                  pl.BlockSpec((tm, D_MODEL), lambda i: (i, gblk + 1)),
                  pl.BlockSpec((tm, D_MODEL), row),
                  pl.BlockSpec((1, N_MOD, D_MODEL), lambda i: (i // tiles_per_batch, 0, 0)),
                  pl.BlockSpec((1, D_MODEL), const),
                  resident((ATTN_WIDTH, D_MODEL)),
                  resident((CONV_CH, D_MODEL)),
                  resident((D_MODEL, D_MODEL)),
                  resident((D_MODEL, LANES)),
                  resident((D_MODEL, LANES)),
                  pl.BlockSpec((1, LANES), const)],
        out_specs=[pl.BlockSpec((tm, D_MODEL), row),
                   pl.BlockSpec((tm, D_MODEL), row),
                   pl.BlockSpec((tm, LANES), row)],
        out_shape=[jax.ShapeDtypeStruct((n, D_MODEL), F32),
                   jax.ShapeDtypeStruct((n, D_MODEL), F32),
                   jax.ShapeDtypeStruct((n, LANES), F32)],
        compiler_params=_params("parallel"),
        name="merge",
    )(o2d, cx2d, px2d, px2d, x2d, m, ng, wap, wcp, wout, wr_hi, wr_lo, rb)


MOE_TM = 256


def _dispatch_plan(route, tm):
    n = route.shape[0]
    n_tiles = (2 * n) // tm + N_EXPERTS
    flat = route[:, 0:2].astype(jnp.int32).reshape(-1)
    onehot = (flat[:, None] == jnp.arange(N_EXPERTS, dtype=jnp.int32)[None, :]).astype(jnp.int32)
    blk = LANES
    tri = (jnp.arange(blk)[:, None] >= jnp.arange(blk)[None, :]).astype(BF16)
    within = jnp.einsum('ij,bjk->bik', tri, onehot.reshape(-1, blk, N_EXPERTS).astype(BF16),
                        preferred_element_type=F32).astype(jnp.int32)
    totals = within[:, -1, :]
    csum = (within + (jnp.cumsum(totals, axis=0) - totals)[:, None, :]).reshape(2 * n, N_EXPERTS)
    rank = jnp.sum((csum - onehot) * onehot, axis=1)
    padded = ((csum[-1] + tm - 1) // tm) * tm
    ends = jnp.cumsum(padded)
    dest = (ends - padded)[flat] + rank
    src = jnp.zeros((n_tiles * tm,), jnp.int32).at[dest].set(jnp.arange(2 * n, dtype=jnp.int32) // 2)
    tile_start = jnp.arange(n_tiles, dtype=jnp.int32) * tm
    tile_expert = jnp.minimum(jnp.sum((tile_start[:, None] >= ends[None, :]).astype(jnp.int32), axis=1),
                              N_EXPERTS - 1)
    n_used = (ends[-1] // tm).reshape(1)
    pos = dest.reshape(n, 2)
    return src.reshape(n_tiles, 1, tm), tile_expert, n_used, pos


def _row_gather(idx_ref, src_hbm, dst_ref, sem, tm):
    for r in range(tm):
        pltpu.make_async_copy(src_hbm.at[pl.ds(idx_ref[0, 0, r], 1)], dst_ref.at[pl.ds(r, 1)], sem).start()


def _row_gather_wait(src_hbm, dst_ref, sem, tm):
    for r in range(tm):
        pltpu.make_async_copy(src_hbm.at[pl.ds(0, 1)], dst_ref.at[pl.ds(r, 1)], sem).wait()


def _expert_kernel(te_ref, nu_ref, src_ref, srcn_ref, t_hbm, w1_ref, w3_ref, w2_ref, y_ref,
                   xa_ref, xb_ref, w1b_ref, w3b_ref, w2b_ref, sem, *, tm):
    t = pl.program_id(0)
    nu = nu_ref[0]
    active = t < nu
    even = t % 2 == 0

    @pl.when(t == 0)
    def _():
        _row_gather(src_ref, t_hbm, xa_ref, sem.at[0], tm)

    @pl.when(active & ((t == 0) | (te_ref[t] != te_ref[jnp.maximum(t - 1, 0)])))
    def _():
        w1b_ref[...] = w1_ref[0].astype(BF16)
        w3b_ref[...] = w3_ref[0].astype(BF16)
        w2b_ref[...] = w2_ref[0].astype(BF16)

    def tile(cur_ref, cur_sem, nxt_ref, nxt_sem):
        _row_gather_wait(t_hbm, cur_ref, cur_sem, tm)
        _row_gather(srcn_ref, t_hbm, nxt_ref, nxt_sem, tm)
        x = cur_ref[...].astype(BF16)
        h1 = jnp.dot(x, w1b_ref[...], preferred_element_type=F32)
        h3 = jnp.dot(x, w3b_ref[...], preferred_element_type=F32)
        hid = (h1 * jax.nn.sigmoid(h1)) * h3
        y_ref[...] = jnp.dot(hid.astype(BF16), w2b_ref[...], preferred_element_type=F32)

    @pl.when(active & even)
    def _():
        tile(xa_ref, sem.at[0], xb_ref, sem.at[1])

    @pl.when(active & jnp.logical_not(even))
    def _():
        tile(xb_ref, sem.at[1], xa_ref, sem.at[0])

    @pl.when((t == nu - 1) & even)
    def _():
        _row_gather_wait(t_hbm, xb_ref, sem.at[1], tm)

    @pl.when((t == nu - 1) & jnp.logical_not(even))
    def _():
        _row_gather_wait(t_hbm, xa_ref, sem.at[0], tm)

    @pl.when(t >= nu)
    def _():
        y_ref[...] = jnp.zeros(y_ref.shape, F32)


def _experts(src, tile_expert, n_used, t2d, w1, w3, w2, *, tm):
    n_tiles = src.shape[0]
    wspec = lambda shape: pl.BlockSpec(shape, lambda t, te, nu: (te[t], 0, 0))
    idx = lambda f: pl.BlockSpec((1, 1, tm), f, memory_space=pltpu.SMEM)
    grid_spec = pltpu.PrefetchScalarGridSpec(
        num_scalar_prefetch=2,
        grid=(n_tiles,),
        in_specs=[idx(lambda t, te, nu: (t, 0, 0)),
                  idx(lambda t, te, nu: (jnp.minimum(t + 1, n_tiles - 1), 0, 0)),
                  pl.BlockSpec(memory_space=pl.ANY),
                  wspec((1, D_MODEL, EXPERT_FF)), wspec((1, D_MODEL, EXPERT_FF)),
                  wspec((1, EXPERT_FF, D_MODEL))],
        out_specs=pl.BlockSpec((tm, D_MODEL), lambda t, te, nu: (t, 0)),
        scratch_shapes=[pltpu.VMEM((tm, D_MODEL), F32), pltpu.VMEM((tm, D_MODEL), F32),
                        pltpu.VMEM((D_MODEL, EXPERT_FF), BF16), pltpu.VMEM((D_MODEL, EXPERT_FF), BF16),
                        pltpu.VMEM((EXPERT_FF, D_MODEL), BF16), pltpu.SemaphoreType.DMA((2,))])
    return pl.pallas_call(
        functools.partial(_expert_kernel, tm=tm),
        grid_spec=grid_spec,
        out_shape=jax.ShapeDtypeStruct((n_tiles * tm, D_MODEL), F32),
        compiler_params=_params("arbitrary"),
        name="experts",
    )(tile_expert, n_used, src, src, t2d, w1, w3, w2)


def _combine_kernel(pa_ref, pb_ref, pan_ref, pbn_ref, y_hbm, x1_ref, route_ref, m_ref, fg_ref, o_ref,
                    ybuf, sem, *, tm, n_tiles):
    i = pl.program_id(0)
    slot = i % 2

    def gather(ia_ref, ib_ref, sl):
        _row_gather(ia_ref, y_hbm, ybuf.at[sl, 0], sem.at[sl], tm)
        _row_gather(ib_ref, y_hbm, ybuf.at[sl, 1], sem.at[sl], tm)

    @pl.when(i == 0)
    def _():
        gather(pa_ref, pb_ref, 0)

    @pl.when(i + 1 < n_tiles)
    def _():
        gather(pan_ref, pbn_ref, 1 - slot)

    _row_gather_wait(y_hbm, ybuf.at[slot, 0], sem.at[slot], tm)
    _row_gather_wait(y_hbm, ybuf.at[slot, 1], sem.at[slot], tm)
    route = route_ref[...]
    y = route[:, 2:3] * ybuf[slot, 0] + route[:, 3:4] * ybuf[slot, 1]
    x2 = x1_ref[...] + m_ref[0, 5:6, :] * y
    ms = jnp.mean(x2 * x2, axis=-1, keepdims=True)
    o_ref[...] = x2 * lax.rsqrt(ms + EPS) * fg_ref[...]


def _combine(pos, y_sorted, x1, route, m, fg, *, rows_per_batch, tm):
    n = x1.shape[0]
    n_tiles = n // tm
    tiles_per_batch = rows_per_batch // tm
    pos_a = pos[:, 0].reshape(n_tiles, 1, tm)
    pos_b = pos[:, 1].reshape(n_tiles, 1, tm)
    cur = lambda i: (i, 0, 0)
    nxt = lambda i: (jnp.minimum(i + 1, n_tiles - 1), 0, 0)
    idx = lambda f: pl.BlockSpec((1, 1, tm), f, memory_space=pltpu.SMEM)
    row = lambda i: (i, 0)
    return pl.pallas_call(
        functools.partial(_combine_kernel, tm=tm, n_tiles=n_tiles),
        grid=(n_tiles,),
        in_specs=[idx(cur), idx(cur), idx(nxt), idx(nxt),
                  pl.BlockSpec(memory_space=pl.ANY),
                  pl.BlockSpec((tm, D_MODEL), row),
                  pl.BlockSpec((tm, LANES), row),
                  pl.BlockSpec((1, N_MOD, D_MODEL), lambda i: (i // tiles_per_batch, 0, 0)),
                  pl.BlockSpec((1, D_MODEL), lambda i: (0, 0))],
        out_specs=pl.BlockSpec((tm, D_MODEL), row),
        out_shape=jax.ShapeDtypeStruct((n, D_MODEL), F32),
        scratch_shapes=[pltpu.VMEM((2, 2, tm, D_MODEL), F32), pltpu.SemaphoreType.DMA((2,))],
        compiler_params=_params("arbitrary"),
        name="combine",
    )(pos_a, pos_b, pos_a, pos_b, y_sorted, x1, route, m, fg)


def kernel(x, c, ctx, c_ctx, w_ada, b_ada, norm_attn_g, norm_ffn_g, w_in, lambda_q1, lambda_k1, lambda_q2, lambda_k2, attn_subln_g, conv_dw_w, conv_dw_b, conv_ln_g, conv_ln_b, w_attn_proj, w_conv_proj, w_out, router_group_w, router_group_b, router_expert_w, router_expert_b, expert_w1, expert_w3, expert_w2, final_norm_g):
    b, s, d = x.shape
    n_ctx = ctx.shape[1]
    depth = w_ada.shape[0]
    assert depth == 1 and d == D_MODEL and b <= 7
    i = 0
    lambda_init = 0.8 - 0.6 * math.exp(-0.3 * i)

    cond = jnp.zeros((8, d), F32).at[:b].set(c).at[b].set(c_ctx)
    mods = _ada(cond, w_ada[i], b_ada[i][None, :])
    m_x = mods[:b].reshape(b, N_MOD, d)
    m_c = mods[b:b + 1].reshape(1, N_MOD, d)

    w_in_bf = w_in[i].astype(BF16)
    tables = _rope_tables(s)
    g_attn = norm_attn_g[i][None, :]
    px = _inproj(x.reshape(b * s, d), m_x, g_attn, w_in_bf, tables,
                 rows_per_batch=s, seq=s, rope_cols=V_OFF, tm=1024, tn=1024)
    pc = _inproj(ctx.reshape(b * n_ctx, d), m_c, g_attn, w_in_bf[:, K_OFF:CONV_OFF], tables,
                 rows_per_batch=b * n_ctx, seq=s, rope_cols=0, tm=n_ctx, tn=1024)
    px3 = px.reshape(b, s, IN_COLS)
    pc3 = pc.reshape(b, n_ctx, CONV_OFF - K_OFF)

    lam_p = jnp.stack([lambda_q1[i], lambda_k1[i], lambda_q2[i], lambda_k2[i]])
    o = _attention(px3, pc3, lam_p, attn_subln_g[i][None, :], lambda_init=lambda_init, tq=256)
    cx = _conv(px3, conv_dw_w[i], conv_dw_b[i][None, :], conv_ln_g[i][None, :], conv_ln_b[i][None, :],
               ts=512)

    wr = jnp.zeros((d, LANES), F32)
    wr = wr.at[:, :N_EXPERTS].set(router_expert_w[i]).at[:, N_EXPERTS:N_EXPERTS + N_GROUPS].set(router_group_w[i])
    rb = jnp.zeros((1, LANES), F32)
    rb = rb.at[0, :N_EXPERTS].set(router_expert_b[i]).at[0, N_EXPERTS:N_EXPERTS + N_GROUPS].set(router_group_b[i])
    wr_hi = wr.astype(BF16)
    wr_lo = (wr - wr_hi.astype(F32)).astype(BF16)

    x1, t, route = _merge(o.reshape(b * s, ATTN_WIDTH), cx.reshape(b * s, CONV_CH), px, x.reshape(b * s, d),
                          m_x, norm_ffn_g[i][None, :],
                          w_attn_proj[i].astype(BF16), w_conv_proj[i].astype(BF16), w_out[i].astype(BF16),
                          wr_hi, wr_lo, rb, rows_per_batch=s, tm=256)

    src, tile_expert, n_used, pos = _dispatch_plan(route, MOE_TM)
    y_sorted = _experts(src, tile_expert, n_used, t, expert_w1[i], expert_w3[i], expert_w2[i], tm=MOE_TM)
    out = _combine(pos, y_sorted, x1, route, m_x, final_norm_g[None, :], rows_per_batch=s, tm=MOE_TM)
    return out.reshape(b, s, d)
```
